```python
import math
import jax, jax.numpy as jnp
from jax import lax
import numpy as np

D_MODEL = 1024
BATCH = 4
SEQ = 4096
DEPTH = 2

GRID_W = 64
HEAD_DIM = 64
ROPE_THETA = 10000.0
EPS = 1e-6
Q_BLOCK = 128
A_HEADS = 6
A_KV_HEADS = 2
A_GROUP = A_HEADS // A_KV_HEADS
B_HEADS = 4
B_Q_RANK = 256
B_KV_RANK = 128
B_NOPE_DIM = 64
B_ROPE_DIM = 32
B_V_DIM = 64
C_HEADS = 6
C_BRANCHES = ((128, 1), (512, 4), (2048, 16))
D_FF = 4 * D_MODEL
N_MOD = 6

IN_SIZES = (A_HEADS * HEAD_DIM, A_KV_HEADS * HEAD_DIM, A_KV_HEADS * HEAD_DIM,
            B_Q_RANK, B_KV_RANK, B_ROPE_DIM,
            C_HEADS * HEAD_DIM, C_HEADS * HEAD_DIM, C_HEADS * HEAD_DIM)
IN_COLS = sum(IN_SIZES)
SPLIT_POINTS = tuple(int(v) for v in np.cumsum(IN_SIZES)[:-1])
A_OUT = A_HEADS * HEAD_DIM
B_OUT = B_HEADS * B_V_DIM
C_OUT = C_HEADS * HEAD_DIM
MIX_WIDTH = A_OUT + B_OUT + C_OUT

kernel_name = "hybrid_gqa_mla_dilated_adaln_encoder"


def rmsnorm(x, g):
    xf = x.astype(jnp.float32)
    y = xf * lax.rsqrt(jnp.mean(xf * xf, axis=-1, keepdims=True) + EPS)
    return (y * g.astype(jnp.float32)).astype(x.dtype)


def rope_angles(pos, dim):
    inv = ROPE_THETA ** (-jnp.arange(0, dim, 2, dtype=jnp.float32) / dim)
    return pos.astype(jnp.float32)[:, None] * inv[None, :]


def apply_rope(x, ang):
    xf = x.astype(jnp.float32)
    x1, x2 = jnp.split(xf, 2, axis=-1)
    cos, sin = jnp.cos(ang), jnp.sin(ang)
    return jnp.concatenate([x1 * cos - x2 * sin, x1 * sin + x2 * cos], axis=-1).astype(x.dtype)


def blocked_attention(q, k, v, scale):
    B, Hk, G, S, Dq = q.shape
    nblk = S // Q_BLOCK
    qb = jnp.moveaxis(q.reshape(B, Hk, G, nblk, Q_BLOCK, Dq), 3, 0)
    kf = k.astype(jnp.float32)

    def one_block(qi):
        s = jnp.einsum('bhgqd,bhkd->bhgqk', qi.astype(jnp.float32), kf) * scale
        p = jax.nn.softmax(s, axis=-1)
        return jnp.einsum('bhgqk,bhkd->bhgqd', p.astype(v.dtype), v)

    ob = lax.map(one_block, qb)
    return jnp.moveaxis(ob, 0, 3).reshape(B, Hk, G, S, v.shape[-1])


def dilated_branch(q, k, v, window, dilation):
    B, H, S, Dh = q.shape
    radius = window // (2 * dilation)
    L = S // dilation

    def strided(a):
        return a.reshape(B, H, L, dilation, Dh).transpose(0, 1, 3, 2, 4)

    qs, ks, vs = strided(q), strided(k), strided(v)
    qb = math.gcd(L, Q_BLOCK)
    nblk = L // qb
    span = qb + 2 * radius
    pad = ((0, 0), (0, 0), (0, 0), (radius, radius), (0, 0))
    kp, vp = jnp.pad(ks, pad), jnp.pad(vs, pad)
    slab_idx = jnp.arange(nblk)[:, None] * qb + jnp.arange(span)[None, :]
    kblk = kp[:, :, :, slab_idx]
    vblk = vp[:, :, :, slab_idx]
    qblk = qs.reshape(B, H, dilation, nblk, qb, Dh)
    s = jnp.einsum('bhrnqd,bhrnkd->bhrnqk', qblk.astype(jnp.float32),
                   kblk.astype(jnp.float32)) * (Dh ** -0.5)
    i_loc = jnp.arange(qb)[:, None]
    j_loc = jnp.arange(span)[None, :]
    band = (j_loc >= i_loc) & (j_loc <= i_loc + 2 * radius)
    valid = (slab_idx >= radius) & (slab_idx < radius + L)
    mask = band[None, :, :] & valid[:, None, :]
    s = jnp.where(mask, s, -jnp.inf)
    lse = jax.nn.logsumexp(s, axis=-1)
    p = jnp.exp(s - lse[..., None])
    o = jnp.einsum('bhrnqk,bhrnkd->bhrnqd', p.astype(v.dtype), vblk)
    o = o.reshape(B, H, dilation, L, Dh).transpose(0, 1, 3, 2, 4).reshape(B, H, S, Dh)
    lse = lse.reshape(B, H, dilation, L).transpose(0, 1, 3, 2).reshape(B, H, S)
    return o, lse


def dilated_mixture(q, k, v):
    outs, lses = [], []
    for window, dilation in C_BRANCHES:
        o, lse = dilated_branch(q, k, v, window, dilation)
        outs.append(o)
        lses.append(lse)
    alpha = jax.nn.softmax(jnp.stack(lses, axis=0), axis=0)
    return jnp.sum(alpha[..., None].astype(v.dtype) * jnp.stack(outs, axis=0), axis=0)


def hybrid_mixer(h, w_in, a_q_norm, a_k_norm, b_q_norm, b_kv_norm, b_w_uq, b_w_ukv,
                 out_gain, ang_axial, ang_mla, ang_dil):
    B, S, _ = h.shape
    proj = h @ w_in
    aq, ak, av, bq, bkv, bkr, cq, ck, cv = jnp.split(proj, SPLIT_POINTS, axis=-1)

    def heads(a, n, d):
        return a.reshape(B, S, n, d).transpose(0, 2, 1, 3)

    def merge(o):
        return o.transpose(0, 2, 1, 3).reshape(B, S, -1)

    qa = apply_rope(rmsnorm(heads(aq, A_HEADS, HEAD_DIM), a_q_norm), ang_axial)
    ka = apply_rope(rmsnorm(heads(ak, A_KV_HEADS, HEAD_DIM), a_k_norm), ang_axial)
    va = heads(av, A_KV_HEADS, HEAD_DIM)
    oa = blocked_attention(qa.reshape(B, A_KV_HEADS, A_GROUP, S, HEAD_DIM), ka, va,
                           HEAD_DIM ** -0.5).reshape(B, A_HEADS, S, HEAD_DIM)

    q_up = heads(rmsnorm(bq, b_q_norm) @ b_w_uq, B_HEADS, B_NOPE_DIM + B_ROPE_DIM)
    q_nope, q_rope = jnp.split(q_up, [B_NOPE_DIM], axis=-1)
    qb_ = jnp.concatenate([q_nope, apply_rope(q_rope, ang_mla)], axis=-1)
    kv_up = heads(rmsnorm(bkv, b_kv_norm) @ b_w_ukv, B_HEADS, B_NOPE_DIM + B_V_DIM)
    k_nope, vb = jnp.split(kv_up, [B_NOPE_DIM], axis=-1)
    k_rope = jnp.broadcast_to(apply_rope(bkr, ang_mla)[:, None], (B, B_HEADS, S, B_ROPE_DIM))
    kb = jnp.concatenate([k_nope, k_rope], axis=-1)
    ob = blocked_attention(qb_[:, :, None], kb, vb, (B_NOPE_DIM + B_ROPE_DIM) ** -0.5)[:, :, 0]

    qc = apply_rope(heads(cq, C_HEADS, HEAD_DIM), ang_dil)
    kc = apply_rope(heads(ck, C_HEADS, HEAD_DIM), ang_dil)
    vc = heads(cv, C_HEADS, HEAD_DIM)
    oc = dilated_mixture(qc, kc, vc)

    g_a, g_b, g_c = jnp.split(out_gain, [A_OUT, A_OUT + B_OUT])
    return jnp.concatenate([rmsnorm(merge(oa), g_a), rmsnorm(merge(ob), g_b),
                            rmsnorm(merge(oc), g_c)], axis=-1)


def setup_inputs(seed: int = 0) -> dict:
    key = jax.random.key(seed)
    ks = jax.random.split(key, 18)

    def nrm(k, shape, s):
        return jax.random.normal(k, shape, jnp.float32) * s

    def gain(k, shape):
        return 1.0 + 0.02 * jax.random.normal(k, shape, jnp.float32)

    return {
        "x": nrm(ks[0], (BATCH, SEQ, D_MODEL), 1.0),
        "c": nrm(ks[1], (BATCH, D_MODEL), 1.0),
        "w_ada": nrm(ks[2], (DEPTH, D_MODEL, N_MOD * D_MODEL), 0.5 * D_MODEL ** -0.5),
        "b_ada": nrm(ks[3], (DEPTH, N_MOD * D_MODEL), 0.02),
        "norm_mix": gain(ks[4], (DEPTH, D_MODEL)),
        "norm_mlp": gain(ks[5], (DEPTH, D_MODEL)),
        "w_in": nrm(ks[6], (DEPTH, D_MODEL, IN_COLS), D_MODEL ** -0.5),
        "a_q_norm": gain(ks[7], (DEPTH, HEAD_DIM)),
        "a_k_norm": gain(ks[8], (DEPTH, HEAD_DIM)),
        "b_q_norm": gain(ks[9], (DEPTH, B_Q_RANK)),
        "b_kv_norm": gain(ks[10], (DEPTH, B_KV_RANK)),
        "b_w_uq": nrm(ks[11], (DEPTH, B_Q_RANK, B_HEADS * (B_NOPE_DIM + B_ROPE_DIM)), B_Q_RANK ** -0.5),
        "b_w_ukv": nrm(ks[12], (DEPTH, B_KV_RANK, B_HEADS * (B_NOPE_DIM + B_V_DIM)), B_KV_RANK ** -0.5),
        "out_gain": gain(ks[13], (DEPTH, MIX_WIDTH)),
        "w_out": nrm(ks[14], (DEPTH, MIX_WIDTH, D_MODEL), MIX_WIDTH ** -0.5),
        "w_ff1": nrm(ks[15], (DEPTH, D_MODEL, D_FF), D_MODEL ** -0.5),
        "w_ff2": nrm(ks[16], (DEPTH, D_FF, D_MODEL), D_FF ** -0.5),
        "final_norm": gain(ks[17], (D_MODEL,)),
    }


def reference(x, c, w_ada, b_ada, norm_mix, norm_mlp, w_in, a_q_norm, a_k_norm, b_q_norm,
              b_kv_norm, b_w_uq, b_w_ukv, out_gain, w_out, w_ff1, w_ff2, final_norm):
    B, S, _ = x.shape
    n_rows = S // GRID_W
    t = jnp.arange(S, dtype=jnp.int32)
    row_idx = jnp.repeat(jnp.arange(n_rows, dtype=jnp.int32), GRID_W)
    col_idx = jnp.tile(jnp.arange(GRID_W, dtype=jnp.int32), n_rows)
    ang_axial = jnp.concatenate([rope_angles(row_idx, HEAD_DIM // 2),
                                 rope_angles(col_idx, HEAD_DIM // 2)], axis=-1)
    ang_mla = rope_angles(t, B_ROPE_DIM)
    ang_dil = rope_angles(t, HEAD_DIM)
    c_act = jax.nn.silu(c)
    for l in range(DEPTH):
        mod = (c_act @ w_ada[l] + b_ada[l])[:, None, :]
        sh1, sc1, g1, sh2, sc2, g2 = jnp.split(mod, N_MOD, axis=-1)
        h = rmsnorm(x, norm_mix[l]) * (1.0 + sc1) + sh1
        y = hybrid_mixer(h, w_in[l], a_q_norm[l], a_k_norm[l], b_q_norm[l], b_kv_norm[l],
                         b_w_uq[l], b_w_ukv[l], out_gain[l], ang_axial, ang_mla, ang_dil)
        x = x + g1 * (y @ w_out[l])
        h = rmsnorm(x, norm_mlp[l]) * (1.0 + sc2) + sh2
        x = x + g2 * (jnp.square(jax.nn.relu(h @ w_ff1[l])) @ w_ff2[l])
    return rmsnorm(x, final_norm)
```

```python
import functools
import math

import numpy as np
import jax
import jax.numpy as jnp
from jax import lax
from jax.experimental import pallas as pl
from jax.experimental.pallas import tpu as pltpu

F32 = jnp.float32
BF16 = jnp.bfloat16

D_MODEL = 1024
GRID_W = 64
HEAD_DIM = 64
ROPE_THETA = 10000.0
EPS = 1e-6
A_HEADS, A_KV_HEADS = 6, 2
A_GROUP = A_HEADS // A_KV_HEADS
B_HEADS, B_Q_RANK, B_KV_RANK = 4, 256, 128
B_NOPE, B_ROPE, B_V = 64, 32, 64
C_HEADS = 6
C_BRANCHES = ((128, 1), (512, 4), (2048, 16))
D_FF = 4 * D_MODEL
N_MOD = 6
A_OUT, B_OUT, C_OUT = A_HEADS * HEAD_DIM, B_HEADS * B_V, C_HEADS * HEAD_DIM

LANES = 128
LOG2E = 1.4426950408889634
NEG_BIG = -1e30
VMEM_LIMIT = 52 * 1024 * 1024

COL_AQ, COL_AK, COL_AV = 0, 384, 512
COL_BQ, COL_BKV, COL_BKR = 640, 896, 1024
COL_CQ, COL_CK, COL_CV = 1152, 1536, 1920
IN_COLS_PADDED = 2304


def _cparams(sem):
    return pltpu.CompilerParams(dimension_semantics=sem, vmem_limit_bytes=VMEM_LIMIT)


def _rms(x):
    return x * lax.rsqrt(jnp.mean(x * x, axis=-1, keepdims=True) + EPS)


def _mod_kernel(c_ref, w_ref, b_ref, o_ref):
    c = c_ref[...]
    c_act = c * (1.0 / (1.0 + jnp.exp(-c)))
    o_ref[...] = jnp.dot(c_act.astype(BF16), w_ref[...].astype(BF16),
                         preferred_element_type=F32) + b_ref[...]


def _modulation(c_pad, w_ada, b_ada):
    depth, d, n = w_ada.shape
    tn = 1536
    return pl.pallas_call(
        _mod_kernel,
        grid=(depth, n // tn),
        in_specs=[pl.BlockSpec((8, d), lambda l, j: (0, 0)),
                  pl.BlockSpec((None, d, tn), lambda l, j: (l, 0, j)),
                  pl.BlockSpec((None, 1, tn), lambda l, j: (l, 0, j))],
        out_specs=pl.BlockSpec((None, 8, tn), lambda l, j: (l, 0, j)),
        out_shape=jax.ShapeDtypeStruct((depth, 8, n), F32),
        compiler_params=_cparams(("arbitrary", "arbitrary")),
        name="modulation",
    )(c_pad, w_ada, b_ada.reshape(depth, 1, n))


def _rope(xb, c, s, first, shift):
    up = pltpu.roll(xb, LANES - shift, 1)
    dn = pltpu.roll(xb, shift, 1)
    return xb * c + jnp.where(first, up, dn) * s


def _pair_rms(xb, g, lo):
    sq = xb * xb
    s_lo = jnp.sum(jnp.where(lo, sq, 0.0), axis=-1, keepdims=True)
    s_hi = jnp.sum(jnp.where(lo, 0.0, sq), axis=-1, keepdims=True)
    ms = jnp.where(lo, s_lo, s_hi) * (1.0 / HEAD_DIM)
    return xb * lax.rsqrt(ms + EPS) * g


def _in_proj_kernel(x_ref, sc_ref, sh_ref, nw_ref, w_ref, gq_ref, gk_ref, gbq_ref, gbkv_ref,
                    wuq_ref, wukv_ref, axc_ref, axs_ref, mlc_ref, mls_ref, dlc_ref, dls_ref,
                    qa_ref, ka_ref, va_ref, qb_ref, kb_ref, vb_ref, qc_ref, kc_ref, vc_ref):
    tm = x_ref.shape[0]
    h = _rms(x_ref[...]) * nw_ref[...]
    h = h * (1.0 + sc_ref[...]) + sh_ref[...]
    proj = jnp.dot(h.astype(BF16), w_ref[...], preferred_element_type=F32)

    lane = lax.broadcasted_iota(jnp.int32, (tm, LANES), 1)
    lo = lane < HEAD_DIM
    first32 = (lane & 32) == 0
    first16 = (lane & 16) == 0
    axc, axs = axc_ref[...], axs_ref[...]
    mlc, mls = mlc_ref[...], mls_ref[...]
    dlc, dls = dlc_ref[...], dls_ref[...]
    scale_a = HEAD_DIM ** -0.5 * LOG2E
    scale_b = (B_NOPE + B_ROPE) ** -0.5 * LOG2E

    gq = gq_ref[...]
    for p in range(A_HEADS // 2):
        xb = proj[:, COL_AQ + p * LANES: COL_AQ + (p + 1) * LANES]
        xb = _rope(_pair_rms(xb, gq, lo), axc, axs, first32, 32) * scale_a
        swapped = pltpu.roll(xb, HEAD_DIM, 1)
        for half in range(2):
            head = 2 * p + half
            kv = head // A_GROUP
            src = xb if kv == half else swapped
            keep = lo if kv == 0 else jnp.logical_not(lo)
            qa_ref[head] = jnp.where(keep, src, 0.0).astype(BF16)
    ka = proj[:, COL_AK:COL_AK + LANES]
    ka_ref[...] = _rope(_pair_rms(ka, gk_ref[...], lo), axc, axs, first32, 32).astype(BF16)
    va_ref[...] = proj[:, COL_AV:COL_AV + LANES].astype(BF16)

    bq = _rms(proj[:, COL_BQ:COL_BQ + B_Q_RANK]) * gbq_ref[...]
    q_up = jnp.dot(bq.astype(BF16), wuq_ref[...], preferred_element_type=F32)
    bkv = _rms(proj[:, COL_BKV:COL_BKV + B_KV_RANK]) * gbkv_ref[...]
    kv_up = jnp.dot(bkv.astype(BF16), wukv_ref[...], preferred_element_type=F32)
    kr = _rope(proj[:, COL_BKR:COL_BKR + LANES], mlc, mls, first16, 16)
    for hd in range(B_HEADS):
        qh = _rope(q_up[:, hd * LANES:(hd + 1) * LANES], mlc, mls, first16, 16) * scale_b
        qb_ref[hd] = qh.astype(BF16)
        kb_ref[hd] = (kv_up[:, hd * LANES:(hd + 1) * LANES] + kr).astype(BF16)
    for p in range(B_HEADS // 2):
        vb_ref[p] = kv_up[:, B_HEADS * LANES + p * LANES: B_HEADS * LANES + (p + 1) * LANES].astype(BF16)

    for p in range(C_HEADS // 2):
        sl = slice(p * LANES, (p + 1) * LANES)
        qc = proj[:, COL_CQ + p * LANES: COL_CQ + (p + 1) * LANES]
        kc = proj[:, COL_CK + p * LANES: COL_CK + (p + 1) * LANES]
        qc_ref[:, sl] = (_rope(qc, dlc, dls, first32, 32) * scale_a).astype(BF16)
        kc_ref[:, sl] = _rope(kc, dlc, dls, first32, 32).astype(BF16)
    vc_ref[...] = proj[:, COL_CV:COL_CV + C_OUT].astype(BF16)


def _in_proj(x, sc, sh, nw, w, gq, gk, gbq, gbkv, wuq, wukv, tables, tm=512):
    b, s, d = x.shape
    row = lambda i, bb: (bb, i, 0)
    const2 = lambda i, bb: (0, 0)
    per_b = lambda i, bb: (bb, 0, 0)
    tab = lambda i, bb: (i, 0)
    headed = lambda i, bb: (bb, 0, i, 0)
    in_specs = [pl.BlockSpec((None, tm, d), row),
                pl.BlockSpec((None, 1, d), per_b), pl.BlockSpec((None, 1, d), per_b),
                pl.BlockSpec((1, d), const2),
                pl.BlockSpec(w.shape, const2),
                pl.BlockSpec((1, LANES), const2), pl.BlockSpec((1, LANES), const2),
                pl.BlockSpec((1, B_Q_RANK), const2), pl.BlockSpec((1, B_KV_RANK), const2),
                pl.BlockSpec(wuq.shape, const2), pl.BlockSpec(wukv.shape, const2)]
    in_specs += [pl.BlockSpec((tm, LANES), tab)] * 6
    out_shape = [jax.ShapeDtypeStruct((b, A_HEADS, s, LANES), BF16),
                 jax.ShapeDtypeStruct((b, s, LANES), BF16),
                 jax.ShapeDtypeStruct((b, s, LANES), BF16),
                 jax.ShapeDtypeStruct((b, B_HEADS, s, LANES), BF16),
                 jax.ShapeDtypeStruct((b, B_HEADS, s, LANES), BF16),
                 jax.ShapeDtypeStruct((b, B_HEADS // 2, s, LANES), BF16),
                 jax.ShapeDtypeStruct((b, s, C_OUT), BF16),
                 jax.ShapeDtypeStruct((b, s, C_OUT), BF16),
                 jax.ShapeDtypeStruct((b, s, C_OUT), BF16)]
    out_specs = [pl.BlockSpec((None, A_HEADS, tm, LANES), headed),
                 pl.BlockSpec((None, tm, LANES), row),
                 pl.BlockSpec((None, tm, LANES), row),
                 pl.BlockSpec((None, B_HEADS, tm, LANES), headed),
                 pl.BlockSpec((None, B_HEADS, tm, LANES), headed),
                 pl.BlockSpec((None, B_HEADS // 2, tm, LANES), headed),
                 pl.BlockSpec((None, tm, C_OUT), row),
                 pl.BlockSpec((None, tm, C_OUT), row),
                 pl.BlockSpec((None, tm, C_OUT), row)]
    return pl.pallas_call(
        _in_proj_kernel,
        grid=(s // tm, b),
        in_specs=in_specs, out_specs=out_specs, out_shape=out_shape,
        compiler_params=_cparams(("arbitrary", "arbitrary")),
        name="in_proj",
    )(x, sc, sh, nw, w, gq, gk, gbq, gbkv, wuq, wukv, *tables)


def _flash_unit(q_parts, k_load, v_load, m_sc, l_sc, acc_sc, n_chunks, tk):
    m_sc[...] = jnp.full(m_sc.shape, NEG_BIG, F32)
    l_sc[...] = jnp.zeros(l_sc.shape, F32)
    acc_sc[...] = jnp.zeros(acc_sc.shape, F32)

    def body(c, carry):
        start = pl.multiple_of(c * tk, tk)
        parts = [lax.dot_general(q, k_load(kid, start), (((1,), (1,)), ((), ())),
                                 preferred_element_type=F32) for q, kid in q_parts]
        s = parts[0] if len(parts) == 1 else jnp.concatenate(parts, axis=0)
        m_prev = m_sc[...]
        m_new = jnp.maximum(m_prev, jnp.max(s, axis=-1, keepdims=True))
        alpha = jnp.exp2(m_prev - m_new)
        p = jnp.exp2(s - m_new)
        l_sc[...] = alpha * l_sc[...] + jnp.sum(p, axis=-1, keepdims=True)
        acc_sc[...] = alpha * acc_sc[...] + jnp.dot(p.astype(BF16), v_load(start),
                                                    preferred_element_type=F32)
        m_sc[...] = m_new
        return carry

    lax.fori_loop(0, n_chunks, body, 0)
    return acc_sc[...] / l_sc[...]


def _attn_a_kernel(q_ref, k_ref, v_ref, o_ref, m_sc, l_sc, acc_sc, *, tk):
    n_heads, tq, _ = q_ref.shape
    s_len = k_ref.shape[0]
    q = q_ref[...].reshape(n_heads * tq, LANES)
    o = _flash_unit([(q, 0)],
                    lambda kid, st: k_ref[pl.ds(st, tk), :],
                    lambda st: v_ref[pl.ds(st, tk), :],
                    m_sc, l_sc, acc_sc, s_len // tk, tk)
    lane = lax.broadcasted_iota(jnp.int32, (tq, LANES), 1)
    lo = lane < HEAD_DIM
    for p in range(n_heads // 2):
        halves = []
        for half in range(2):
            head = 2 * p + half
            oh = o[head * tq:(head + 1) * tq]
            halves.append(oh if head // A_GROUP == half else pltpu.roll(oh, HEAD_DIM, 1))
        o_ref[:, p * LANES:(p + 1) * LANES] = jnp.where(lo, halves[0], halves[1])


def _attn_a(qa, ka, va, tq=128, tk=512):
    b, nh, s, _ = qa.shape
    m = nh * tq
    return pl.pallas_call(
        functools.partial(_attn_a_kernel, tk=tk),
        grid=(b, s // tq),
        in_specs=[pl.BlockSpec((None, nh, tq, LANES), lambda bb, i: (bb, 0, i, 0)),
                  pl.BlockSpec((None, s, LANES), lambda bb, i: (bb, 0, 0)),
                  pl.BlockSpec((None, s, LANES), lambda bb, i: (bb, 0, 0))],
        out_specs=pl.BlockSpec((None, tq, A_OUT), lambda bb, i: (bb, i, 0)),
        out_shape=jax.ShapeDtypeStruct((b, s, A_OUT), F32),
        scratch_shapes=[pltpu.VMEM((m, 1), F32), pltpu.VMEM((m, 1), F32),
                        pltpu.VMEM((m, LANES), F32)],
        compiler_params=_cparams(("arbitrary", "arbitrary")),
        name="attn_a",
    )(qa, ka, va)


def _attn_b_kernel(q_ref, k_ref, v_ref, o_ref, m_sc, l_sc, acc_sc, *, tk):
    n_heads, tq, _ = q_ref.shape
    s_len = k_ref.shape[1]
    lane = lax.broadcasted_iota(jnp.int32, (tq, LANES), 1)
    lo = lane < B_V
    for p in range(n_heads // 2):
        o = _flash_unit([(q_ref[2 * p], 2 * p), (q_ref[2 * p + 1], 2 * p + 1)],
                        lambda kid, st: k_ref[kid, pl.ds(st, tk), :],
                        lambda st: v_ref[p, pl.ds(st, tk), :],
                        m_sc, l_sc, acc_sc, s_len // tk, tk)
        o_ref[:, p * LANES:(p + 1) * LANES] = jnp.where(lo, o[:tq], o[tq:])


def _attn_b(qb, kb, vb, tq=256, tk=512):
    b, nh, s, _ = qb.shape
    m = 2 * tq
    return pl.pallas_call(
        functools.partial(_attn_b_kernel, tk=tk),
        grid=(b, s // tq),
        in_specs=[pl.BlockSpec((None, nh, tq, LANES), lambda bb, i: (bb, 0, i, 0)),
                  pl.BlockSpec((None, nh, s, LANES), lambda bb, i: (bb, 0, 0, 0)),
                  pl.BlockSpec((None, nh // 2, s, LANES), lambda bb, i: (bb, 0, 0, 0))],
        out_specs=pl.BlockSpec((None, tq, B_OUT), lambda bb, i: (bb, i, 0)),
        out_shape=jax.ShapeDtypeStruct((b, s, B_OUT), F32),
        scratch_shapes=[pltpu.VMEM((m, 1), F32), pltpu.VMEM((m, 1), F32),
                        pltpu.VMEM((m, LANES), F32)],
        compiler_params=_cparams(("arbitrary", "arbitrary")),
        name="attn_b",
    )(qb, kb, vb)


DIL_TQ = 128
DIL_RADIUS = 64
DIL_SPAN = DIL_TQ + 2 * DIL_RADIUS


def _dilated_kernel(q_ref, k_ref, v_ref, o_ref, lse_ref):
    seq = k_ref.shape[0]
    i = pl.program_id(2)
    l0 = i * DIL_TQ
    ws = pl.multiple_of(jnp.clip(l0 - DIL_RADIUS, 0, seq - DIL_SPAN), DIL_RADIUS)
    row = lax.broadcasted_iota(jnp.int32, (2 * DIL_TQ, DIL_SPAN), 0)
    col = lax.broadcasted_iota(jnp.int32, (2 * DIL_TQ, DIL_SPAN), 1)
    delta = (col + (ws - l0)) - (row & (DIL_TQ - 1))
    band = (delta >= -DIL_RADIUS) & (delta <= DIL_RADIUS)
    lane = lax.broadcasted_iota(jnp.int32, (DIL_TQ, LANES), 1)
    lo = lane < HEAD_DIM
    zero = jnp.zeros((DIL_TQ, LANES), BF16)
    for p in range(C_HEADS // 2):
        sl = slice(p * LANES, (p + 1) * LANES)
        q2 = q_ref[:, sl]
        qs = jnp.concatenate([jnp.where(lo, q2, zero), jnp.where(lo, zero, q2)], axis=0)
        kw = k_ref[pl.ds(ws, DIL_SPAN), sl]
        vw = v_ref[pl.ds(ws, DIL_SPAN), sl]
        s = lax.dot_general(qs, kw, (((1,), (1,)), ((), ())), preferred_element_type=F32)
        s = jnp.where(band, s, NEG_BIG)
        m = jnp.max(s, axis=-1, keepdims=True)
        pr = jnp.exp2(s - m)
        l = jnp.sum(pr, axis=-1, keepdims=True)
        o = jnp.dot(pr.astype(BF16), vw, preferred_element_type=F32) / l
        lse2 = jnp.broadcast_to(m + jnp.log2(l), (2 * DIL_TQ, LANES))
        o_ref[:, sl] = jnp.where(lo, o[:DIL_TQ], o[DIL_TQ:])
        lse_ref[:, sl] = jnp.where(lo, lse2[:DIL_TQ], lse2[DIL_TQ:])


def _dilated_branch(qc, kc, vc, dilation):
    b, s, c = qc.shape
    seq = s // dilation
    view = lambda a: a.reshape(b, seq, dilation * c)
    o, lse = pl.pallas_call(
        _dilated_kernel,
        grid=(b, dilation, seq // DIL_TQ),
        in_specs=[pl.BlockSpec((None, DIL_TQ, c), lambda bb, r, i: (bb, i, r)),
                  pl.BlockSpec((None, seq, c), lambda bb, r, i: (bb, 0, r)),
                  pl.BlockSpec((None, seq, c), lambda bb, r, i: (bb, 0, r))],
        out_specs=[pl.BlockSpec((None, DIL_TQ, c), lambda bb, r, i: (bb, i, r)),
                   pl.BlockSpec((None, DIL_TQ, c), lambda bb, r, i: (bb, i, r))],
        out_shape=[jax.ShapeDtypeStruct((b, seq, dilation * c), F32),
                   jax.ShapeDtypeStruct((b, seq, dilation * c), F32)],
        compiler_params=_cparams(("arbitrary", "arbitrary", "arbitrary")),
        name=f"dilated_d{dilation}",
    )(view(qc), view(kc), view(vc))
    return o.reshape(b, s, c), lse.reshape(b, s, c)


def _out_proj_kernel(x_ref, oa_ref, ob_ref, o1_ref, o2_ref, o3_ref, l1_ref, l2_ref, l3_ref,
                     g_ref, w_ref, gate_ref, out_ref):
    l1, l2, l3 = l1_ref[...], l2_ref[...], l3_ref[...]
    mx = jnp.maximum(jnp.maximum(l1, l2), l3)
    w1, w2, w3 = jnp.exp2(l1 - mx), jnp.exp2(l2 - mx), jnp.exp2(l3 - mx)
    tot = w1 + w2 + w3
    oc = (w1 / tot) * o1_ref[...] + (w2 / tot) * o2_ref[...] + (w3 / tot) * o3_ref[...]
    g = g_ref[...]
    y = jnp.concatenate([(_rms(oa_ref[...]) * g[:, :A_OUT]).astype(BF16),
                         (_rms(ob_ref[...]) * g[:, A_OUT:A_OUT + B_OUT]).astype(BF16),
                         (_rms(oc) * g[:, A_OUT + B_OUT:]).astype(BF16)], axis=-1)
    out_ref[...] = x_ref[...] + gate_ref[...] * jnp.dot(y, w_ref[...], preferred_element_type=F32)


def _out_proj(x, oa, ob, o_c, lse_c, gain, w, gate, tm=512):
    b, s, d = x.shape
    row = lambda bb, i: (bb, i, 0)
    const2 = lambda bb, i: (0, 0)
    per_b = lambda bb, i: (bb, 0, 0)
    cspec = pl.BlockSpec((None, tm, C_OUT), row)
    return pl.pallas_call(
        _out_proj_kernel,
        grid=(b, s // tm),
        in_specs=[pl.BlockSpec((None, tm, d), row),
                  pl.BlockSpec((None, tm, A_OUT), row), pl.BlockSpec((None, tm, B_OUT), row),
                  cspec, cspec, cspec, cspec, cspec, cspec,
                  pl.BlockSpec((1, d), const2), pl.BlockSpec(w.shape, const2),
                  pl.BlockSpec((None, 1, d), per_b)],
        out_specs=pl.BlockSpec((None, tm, d), row),
        out_shape=jax.ShapeDtypeStruct((b, s, d), F32),
        compiler_params=_cparams(("arbitrary", "arbitrary")),
        name="out_proj",
    )(x, oa, ob, *o_c, *lse_c, gain, w, gate)


def _mlp_kernel(x_ref, sc_ref, sh_ref, nw_ref, w1_ref, w2_ref, gate_ref, fn_ref, out_ref,
                h_sc, acc_sc, *, final_norm):
    k = pl.program_id(2)

    @pl.when(k == 0)
    def _():
        h = _rms(x_ref[...]) * nw_ref[...]
        h_sc[...] = (h * (1.0 + sc_ref[...]) + sh_ref[...]).astype(BF16)
        acc_sc[...] = jnp.zeros(acc_sc.shape, F32)

    a = jnp.dot(h_sc[...], w1_ref[...], preferred_element_type=F32)
    a = jnp.square(jnp.maximum(a, 0.0))
    acc_sc[...] += jnp.dot(a.astype(BF16), w2_ref[...], preferred_element_type=F32)

    @pl.when(k == pl.num_programs(2) - 1)
    def _():
        y = x_ref[...] + gate_ref[...] * acc_sc[...]
        if final_norm:
            y = _rms(y) * fn_ref[...]
        out_ref[...] = y


def _mlp(x, sc, sh, nw, w1, w2, gate, fn, final_norm, tm=1024, fc=1024):
    b, s, d = x.shape
    ff = w1.shape[1]
    row = lambda bb, i, k: (bb, i, 0)
    const2 = lambda bb, i, k: (0, 0)
    per_b = lambda bb, i, k: (bb, 0, 0)
    return pl.pallas_call(
        functools.partial(_mlp_kernel, final_norm=final_norm),
        grid=(b, s // tm, ff // fc),
        in_specs=[pl.BlockSpec((None, tm, d), row),
                  pl.BlockSpec((None, 1, d), per_b), pl.BlockSpec((None, 1, d), per_b),
                  pl.BlockSpec((1, d), const2),
                  pl.BlockSpec((d, fc), lambda bb, i, k: (0, k)),
                  pl.BlockSpec((fc, d), lambda bb, i, k: (k, 0)),
                  pl.BlockSpec((None, 1, d), per_b),
                  pl.BlockSpec((1, d), const2)],
        out_specs=pl.BlockSpec((None, tm, d), row),
        out_shape=jax.ShapeDtypeStruct((b, s, d), F32),
        scratch_shapes=[pltpu.VMEM((tm, d), BF16), pltpu.VMEM((tm, d), F32)],
        compiler_params=_cparams(("arbitrary", "arbitrary", "arbitrary")),
        name="mlp",
    )(x, sc, sh, nw, w1, w2, gate, fn)


def _rope_tables(s):
    def angles(pos, dim):
        inv = ROPE_THETA ** (-jnp.arange(0, dim, 2, dtype=F32) / dim)
        return pos.astype(F32)[:, None] * inv[None, :]

    t = jnp.arange(s, dtype=jnp.int32)
    n_rows = s // GRID_W
    row_idx = jnp.repeat(jnp.arange(n_rows, dtype=jnp.int32), GRID_W)
    col_idx = jnp.tile(jnp.arange(GRID_W, dtype=jnp.int32), n_rows)
    ang_axial = jnp.concatenate([angles(row_idx, HEAD_DIM // 2), angles(col_idx, HEAD_DIM // 2)], axis=-1)
    ang_mla = angles(t, B_ROPE)
    ang_dil = angles(t, HEAD_DIM)

    def head64(ang):
        c, sn = jnp.cos(ang), jnp.sin(ang)
        return jnp.tile(c, (1, 4)), jnp.tile(jnp.concatenate([-sn, sn], axis=-1), (1, 2))

    c, sn = jnp.cos(ang_mla), jnp.sin(ang_mla)
    ones, zeros32 = jnp.ones((s, B_NOPE), F32), jnp.zeros((s, LANES - B_NOPE - B_ROPE), F32)
    mlc = jnp.concatenate([ones, c, c, zeros32], axis=-1)
    mls = jnp.concatenate([jnp.zeros((s, B_NOPE), F32), -sn, sn, zeros32], axis=-1)
    axc, axs = head64(ang_axial)
    dlc, dls = head64(ang_dil)
    return axc, axs, mlc, mls, dlc, dls


def _arrange_w_in(w_in_l):
    d = w_in_l.shape[0]
    sizes = (A_OUT, 128, 128, B_Q_RANK, B_KV_RANK, B_ROPE, C_OUT, C_OUT, C_OUT)
    offs = np.cumsum((0,) + sizes)
    aq, ak, av, bq, bkv, bkr, cq, ck, cv = [w_in_l[:, offs[j]:offs[j + 1]] for j in range(9)]
    bkr_pad = jnp.concatenate([jnp.zeros((d, B_NOPE), F32), bkr,
                               jnp.zeros((d, LANES - B_NOPE - B_ROPE), F32)], axis=-1)
    return jnp.concatenate([aq, ak, av, bq, bkv, bkr_pad, cq, ck, cv], axis=-1).astype(BF16)


def _arrange_w_uq(w):
    r = w.shape[0]
    w = w.reshape(r, B_HEADS, B_NOPE + B_ROPE)
    w = jnp.concatenate([w, jnp.zeros((r, B_HEADS, LANES - B_NOPE - B_ROPE), F32)], axis=-1)
    return w.reshape(r, B_HEADS * LANES).astype(BF16)


def _arrange_w_ukv(w):
    r = w.shape[0]
    w = w.reshape(r, B_HEADS, B_NOPE + B_V)
    k = jnp.concatenate([w[:, :, :B_NOPE], jnp.zeros((r, B_HEADS, LANES - B_NOPE), F32)], axis=-1)
    v = w[:, :, B_NOPE:]
    return jnp.concatenate([k.reshape(r, B_HEADS * LANES), v.reshape(r, B_HEADS * B_V)],
                           axis=-1).astype(BF16)


def kernel(x, c, w_ada, b_ada, norm_mix, norm_mlp, w_in, a_q_norm, a_k_norm, b_q_norm, b_kv_norm,
           b_w_uq, b_w_ukv, out_gain, w_out, w_ff1, w_ff2, final_norm):
    b, s, d = x.shape
    depth = w_ada.shape[0]
    tables = _rope_tables(s)
    c_pad = jnp.concatenate([c, jnp.zeros((8 - b, d), F32)], axis=0)
    mod = _modulation(c_pad, w_ada, b_ada)[:, :b]
    fn = final_norm.reshape(1, d)
    for l in range(depth):
        sh1, sc1, g1, sh2, sc2, g2 = [mod[l, :, j * d:(j + 1) * d].reshape(b, 1, d) for j in range(N_MOD)]
        qa, ka, va, qb, kb, vb, qc, kc, vc = _in_proj(
            x, sc1, sh1, norm_mix[l].reshape(1, d), _arrange_w_in(w_in[l]),
            jnp.tile(a_q_norm[l], 2).reshape(1, LANES), jnp.tile(a_k_norm[l], 2).reshape(1, LANES),
            b_q_norm[l].reshape(1, B_Q_RANK), b_kv_norm[l].reshape(1, B_KV_RANK),
            _arrange_w_uq(b_w_uq[l]), _arrange_w_ukv(b_w_ukv[l]), tables)
        oa = _attn_a(qa, ka, va)
        ob = _attn_b(qb, kb, vb)
        branches = [_dilated_branch(qc, kc, vc, dil) for _, dil in C_BRANCHES]
        x = _out_proj(x, oa, ob, [o for o, _ in branches], [e for _, e in branches],
                      out_gain[l].reshape(1, d), w_out[l].astype(BF16), g1)
        x = _mlp(x, sc2, sh2, norm_mlp[l].reshape(1, d), w_ff1[l].astype(BF16), w_ff2[l].astype(BF16),
                 g2, fn, final_norm=(l == depth - 1))
    return x
```

```python
import functools
import math

import numpy as np
import jax
import jax.numpy as jnp
from jax import lax
from jax.experimental import pallas as pl
from jax.experimental.pallas import tpu as pltpu

F32 = jnp.float32
BF16 = jnp.bfloat16

D_MODEL = 1024
GRID_W = 64
HEAD_DIM = 64
ROPE_THETA = 10000.0
EPS = 1e-6
A_HEADS, A_KV_HEADS = 6, 2
A_GROUP = A_HEADS // A_KV_HEADS
B_HEADS, B_Q_RANK, B_KV_RANK = 4, 256, 128
B_NOPE, B_ROPE, B_V = 64, 32, 64
C_HEADS = 6
C_BRANCHES = ((128, 1), (512, 4), (2048, 16))
D_FF = 4 * D_MODEL
N_MOD = 6
A_OUT, B_OUT, C_OUT = A_HEADS * HEAD_DIM, B_HEADS * B_V, C_HEADS * HEAD_DIM

LANES = 128
LOG2E = 1.4426950408889634
NEG_BIG = -1e30
VMEM_LIMIT = 52 * 1024 * 1024

COL_AQ, COL_AK, COL_AV = 0, 384, 512
COL_BQ, COL_BKV, COL_BKR = 640, 896, 1024
COL_CQ, COL_CK, COL_CV = 1152, 1536, 1920
IN_COLS_PADDED = 2304


def _cparams(sem):
    return pltpu.CompilerParams(dimension_semantics=sem, vmem_limit_bytes=VMEM_LIMIT)


def _rms(x):
    return x * lax.rsqrt(jnp.mean(x * x, axis=-1, keepdims=True) + EPS)


def _mod_kernel(c_ref, w_ref, b_ref, o_ref):
    c = c_ref[...]
    c_act = c * (1.0 / (1.0 + jnp.exp(-c)))
    o_ref[...] = jnp.dot(c_act.astype(BF16), w_ref[...].astype(BF16),
                         preferred_element_type=F32) + b_ref[...]


def _modulation(c_pad, w_ada, b_ada):
    depth, d, n = w_ada.shape
    tn = 1536
    return pl.pallas_call(
        _mod_kernel,
        grid=(depth, n // tn),
        in_specs=[pl.BlockSpec((8, d), lambda l, j: (0, 0)),
                  pl.BlockSpec((None, d, tn), lambda l, j: (l, 0, j)),
                  pl.BlockSpec((None, 1, tn), lambda l, j: (l, 0, j))],
        out_specs=pl.BlockSpec((None, 8, tn), lambda l, j: (l, 0, j)),
        out_shape=jax.ShapeDtypeStruct((depth, 8, n), F32),
        compiler_params=_cparams(("arbitrary", "arbitrary")),
        name="modulation",
    )(c_pad, w_ada, b_ada.reshape(depth, 1, n))


def _rope(xb, c, s, first, shift):
    up = pltpu.roll(xb, LANES - shift, 1)
    dn = pltpu.roll(xb, shift, 1)
    return xb * c + jnp.where(first, up, dn) * s


def _pair_rms(xb, g, lo):
    sq = xb * xb
    s_lo = jnp.sum(jnp.where(lo, sq, 0.0), axis=-1, keepdims=True)
    s_hi = jnp.sum(jnp.where(lo, 0.0, sq), axis=-1, keepdims=True)
    ms = jnp.where(lo, s_lo, s_hi) * (1.0 / HEAD_DIM)
    return xb * lax.rsqrt(ms + EPS) * g


def _in_proj_kernel(x_ref, sc_ref, sh_ref, nw_ref, w_ref, gq_ref, gk_ref, gbq_ref, gbkv_ref,
                    wuq_ref, wukv_ref, axc_ref, axs_ref, mlc_ref, mls_ref, dlc_ref, dls_ref,
                    qa_ref, ka_ref, va_ref, qb_ref, kb_ref, vb_ref, qc_ref, kc_ref, vc_ref):
    tm = x_ref.shape[0]
    h = _rms(x_ref[...]) * nw_ref[...]
    h = h * (1.0 + sc_ref[...]) + sh_ref[...]
    proj = jnp.dot(h.astype(BF16), w_ref[...], preferred_element_type=F32)

    lane = lax.broadcasted_iota(jnp.int32, (tm, LANES), 1)
    lo = lane < HEAD_DIM
    first32 = (lane & 32) == 0
    first16 = (lane & 16) == 0
    axc, axs = axc_ref[...], axs_ref[...]
    mlc, mls = mlc_ref[...], mls_ref[...]
    dlc, dls = dlc_ref[...], dls_ref[...]
    scale_a = HEAD_DIM ** -0.5 * LOG2E
    scale_b = (B_NOPE + B_ROPE) ** -0.5 * LOG2E

    gq = gq_ref[...]
    for p in range(A_HEADS // 2):
        xb = proj[:, COL_AQ + p * LANES: COL_AQ + (p + 1) * LANES]
        xb = _rope(_pair_rms(xb, gq, lo), axc, axs, first32, 32) * scale_a
        swapped = pltpu.roll(xb, HEAD_DIM, 1)
        for half in range(2):
            head = 2 * p + half
            kv = head // A_GROUP
            src = xb if kv == half else swapped
            keep = lo if kv == 0 else jnp.logical_not(lo)
            qa_ref[head] = jnp.where(keep, src, 0.0).astype(BF16)
    ka = proj[:, COL_AK:COL_AK + LANES]
    ka_ref[...] = _rope(_pair_rms(ka, gk_ref[...], lo), axc, axs, first32, 32).astype(BF16)
    va_ref[...] = proj[:, COL_AV:COL_AV + LANES].astype(BF16)

    bq = _rms(proj[:, COL_BQ:COL_BQ + B_Q_RANK]) * gbq_ref[...]
    q_up = jnp.dot(bq.astype(BF16), wuq_ref[...], preferred_element_type=F32)
    bkv = _rms(proj[:, COL_BKV:COL_BKV + B_KV_RANK]) * gbkv_ref[...]
    kv_up = jnp.dot(bkv.astype(BF16), wukv_ref[...], preferred_element_type=F32)
    kr = _rope(proj[:, COL_BKR:COL_BKR + LANES], mlc, mls, first16, 16)
    for hd in range(B_HEADS):
        qh = _rope(q_up[:, hd * LANES:(hd + 1) * LANES], mlc, mls, first16, 16) * scale_b
        qb_ref[hd] = qh.astype(BF16)
        kb_ref[hd] = (kv_up[:, hd * LANES:(hd + 1) * LANES] + kr).astype(BF16)
    for p in range(B_HEADS // 2):
        vb_ref[p] = kv_up[:, B_HEADS * LANES + p * LANES: B_HEADS * LANES + (p + 1) * LANES].astype(BF16)

    for p in range(C_HEADS // 2):
        sl = slice(p * LANES, (p + 1) * LANES)
        qc = proj[:, COL_CQ + p * LANES: COL_CQ + (p + 1) * LANES]
        kc = proj[:, COL_CK + p * LANES: COL_CK + (p + 1) * LANES]
        qc_ref[:, sl] = (_rope(qc, dlc, dls, first32, 32) * scale_a).astype(BF16)
        kc_ref[:, sl] = _rope(kc, dlc, dls, first32, 32).astype(BF16)
    vc_ref[...] = proj[:, COL_CV:COL_CV + C_OUT].astype(BF16)


def _in_proj(x, sc, sh, nw, w, gq, gk, gbq, gbkv, wuq, wukv, tables, tm=512):
    b, s, d = x.shape
    row = lambda i, bb: (bb, i, 0)
    const2 = lambda i, bb: (0, 0)
    per_b = lambda i, bb: (bb, 0, 0)
    tab = lambda i, bb: (i, 0)
    headed = lambda i, bb: (bb, 0, i, 0)
    in_specs = [pl.BlockSpec((None, tm, d), row),
                pl.BlockSpec((None, 1, d), per_b), pl.BlockSpec((None, 1, d), per_b),
                pl.BlockSpec((1, d), const2),
                pl.BlockSpec(w.shape, const2),
                pl.BlockSpec((1, LANES), const2), pl.BlockSpec((1, LANES), const2),
                pl.BlockSpec((1, B_Q_RANK), const2), pl.BlockSpec((1, B_KV_RANK), const2),
                pl.BlockSpec(wuq.shape, const2), pl.BlockSpec(wukv.shape, const2)]
    in_specs += [pl.BlockSpec((tm, LANES), tab)] * 6
    out_shape = [jax.ShapeDtypeStruct((b, A_HEADS, s, LANES), BF16),
                 jax.ShapeDtypeStruct((b, s, LANES), BF16),
                 jax.ShapeDtypeStruct((b, s, LANES), BF16),
                 jax.ShapeDtypeStruct((b, B_HEADS, s, LANES), BF16),
                 jax.ShapeDtypeStruct((b, B_HEADS, s, LANES), BF16),
                 jax.ShapeDtypeStruct((b, B_HEADS // 2, s, LANES), BF16),
                 jax.ShapeDtypeStruct((b, s, C_OUT), BF16),
                 jax.ShapeDtypeStruct((b, s, C_OUT), BF16),
                 jax.ShapeDtypeStruct((b, s, C_OUT), BF16)]
    out_specs = [pl.BlockSpec((None, A_HEADS, tm, LANES), headed),
                 pl.BlockSpec((None, tm, LANES), row),
                 pl.BlockSpec((None, tm, LANES), row),
                 pl.BlockSpec((None, B_HEADS, tm, LANES), headed),
                 pl.BlockSpec((None, B_HEADS, tm, LANES), headed),
                 pl.BlockSpec((None, B_HEADS // 2, tm, LANES), headed),
                 pl.BlockSpec((None, tm, C_OUT), row),
                 pl.BlockSpec((None, tm, C_OUT), row),
                 pl.BlockSpec((None, tm, C_OUT), row)]
    return pl.pallas_call(
        _in_proj_kernel,
        grid=(s // tm, b),
        in_specs=in_specs, out_specs=out_specs, out_shape=out_shape,
        compiler_params=_cparams(("arbitrary", "arbitrary")),
        name="in_proj",
    )(x, sc, sh, nw, w, gq, gk, gbq, gbkv, wuq, wukv, *tables)


def _softmax_unit(q_parts, k_load, v_load, s_sc, m_sc, l_sc, acc_sc, n_chunks, tk):
    n_col = tk // LANES
    m_sc[...] = jnp.full(m_sc.shape, NEG_BIG, F32)

    def scores(c, carry):
        start = pl.multiple_of(c * tk, tk)
        row = 0
        for q, kid in q_parts:
            s = lax.dot_general(q, k_load(kid, start), (((1,), (1,)), ((), ())),
                                preferred_element_type=F32)
            rows = slice(row, row + q.shape[0])
            s_sc[c, rows, :] = s
            mx = m_sc[rows, :]
            for j in range(n_col):
                mx = jnp.maximum(mx, s[:, j * LANES:(j + 1) * LANES])
            m_sc[rows, :] = mx
            row += q.shape[0]
        return carry

    lax.fori_loop(0, n_chunks, scores, 0)
    m_sc[...] = jnp.broadcast_to(jnp.max(m_sc[...], axis=-1, keepdims=True), m_sc.shape)
    l_sc[...] = jnp.zeros(l_sc.shape, F32)
    acc_sc[...] = jnp.zeros(acc_sc.shape, F32)

    def weighted(c, carry):
        start = pl.multiple_of(c * tk, tk)
        m = m_sc[...]
        ps = [jnp.exp2(s_sc[c, :, j * LANES:(j + 1) * LANES] - m) for j in range(n_col)]
        lsum = ps[0]
        for pj in ps[1:]:
            lsum = lsum + pj
        l_sc[...] += lsum
        p = jnp.concatenate([pj.astype(BF16) for pj in ps], axis=-1)
        acc_sc[...] += jnp.dot(p, v_load(start), preferred_element_type=F32)
        return carry

    lax.fori_loop(0, n_chunks, weighted, 0)
    return acc_sc[...] / jnp.sum(l_sc[...], axis=-1, keepdims=True)


def _attn_scratch(m, s_len, tk):
    return [pltpu.VMEM((s_len // tk, m, tk), F32), pltpu.VMEM((m, LANES), F32),
            pltpu.VMEM((m, LANES), F32), pltpu.VMEM((m, LANES), F32)]


def _attn_a_kernel(q_ref, k_ref, v_ref, o_ref, s_sc, m_sc, l_sc, acc_sc, *, tk):
    n_heads, tq, _ = q_ref.shape
    s_len = k_ref.shape[0]
    q = q_ref[...].reshape(n_heads * tq, LANES)
    o = _softmax_unit([(q, 0)],
                      lambda kid, st: k_ref[pl.ds(st, tk), :],
                      lambda st: v_ref[pl.ds(st, tk), :],
                      s_sc, m_sc, l_sc, acc_sc, s_len // tk, tk)
    lane = lax.broadcasted_iota(jnp.int32, (tq, LANES), 1)
    lo = lane < HEAD_DIM
    for p in range(n_heads // 2):
        halves = []
        for half in range(2):
            head = 2 * p + half
            oh = o[head * tq:(head + 1) * tq]
            halves.append(oh if head // A_GROUP == half else pltpu.roll(oh, HEAD_DIM, 1))
        o_ref[:, p * LANES:(p + 1) * LANES] = jnp.where(lo, halves[0], halves[1])


def _attn_a(qa, ka, va, tq=128, tk=512):
    b, nh, s, _ = qa.shape
    return pl.pallas_call(
        functools.partial(_attn_a_kernel, tk=tk),
        grid=(b, s // tq),
        in_specs=[pl.BlockSpec((None, nh, tq, LANES), lambda bb, i: (bb, 0, i, 0)),
                  pl.BlockSpec((None, s, LANES), lambda bb, i: (bb, 0, 0)),
                  pl.BlockSpec((None, s, LANES), lambda bb, i: (bb, 0, 0))],
        out_specs=pl.BlockSpec((None, tq, A_OUT), lambda bb, i: (bb, i, 0)),
        out_shape=jax.ShapeDtypeStruct((b, s, A_OUT), F32),
        scratch_shapes=_attn_scratch(nh * tq, s, tk),
        compiler_params=_cparams(("arbitrary", "arbitrary")),
        name="attn_a",
    )(qa, ka, va)


def _attn_b_kernel(q_ref, k_ref, v_ref, o_ref, s_sc, m_sc, l_sc, acc_sc, *, tk):
    n_heads, tq, _ = q_ref.shape
    s_len = k_ref.shape[1]
    lane = lax.broadcasted_iota(jnp.int32, (tq, LANES), 1)
    lo = lane < B_V
    for p in range(n_heads // 2):
        o = _softmax_unit([(q_ref[2 * p], 2 * p), (q_ref[2 * p + 1], 2 * p + 1)],
                          lambda kid, st: k_ref[kid, pl.ds(st, tk), :],
                          lambda st: v_ref[p, pl.ds(st, tk), :],
                          s_sc, m_sc, l_sc, acc_sc, s_len // tk, tk)
        o_ref[:, p * LANES:(p + 1) * LANES] = jnp.where(lo, o[:tq], o[tq:])


def _attn_b(qb, kb, vb, tq=256, tk=512):
    b, nh, s, _ = qb.shape
    return pl.pallas_call(
        functools.partial(_attn_b_kernel, tk=tk),
        grid=(b, s // tq),
        in_specs=[pl.BlockSpec((None, nh, tq, LANES), lambda bb, i: (bb, 0, i, 0)),
                  pl.BlockSpec((None, nh, s, LANES), lambda bb, i: (bb, 0, 0, 0)),
                  pl.BlockSpec((None, nh // 2, s, LANES), lambda bb, i: (bb, 0, 0, 0))],
        out_specs=pl.BlockSpec((None, tq, B_OUT), lambda bb, i: (bb, i, 0)),
        out_shape=jax.ShapeDtypeStruct((b, s, B_OUT), F32),
        scratch_shapes=_attn_scratch(2 * tq, s, tk),
        compiler_params=_cparams(("arbitrary", "arbitrary")),
        name="attn_b",
    )(qb, kb, vb)


DIL_TQ = 128
DIL_RADIUS = 64
DIL_SPAN = DIL_TQ + 2 * DIL_RADIUS


def _dilated_kernel(q_ref, k_ref, v_ref, o_ref, lse_ref):
    seq = k_ref.shape[0]
    i = pl.program_id(2)
    l0 = i * DIL_TQ
    ws = pl.multiple_of(jnp.clip(l0 - DIL_RADIUS, 0, seq - DIL_SPAN), DIL_RADIUS)
    row = lax.broadcasted_iota(jnp.int32, (2 * DIL_TQ, DIL_SPAN), 0)
    col = lax.broadcasted_iota(jnp.int32, (2 * DIL_TQ, DIL_SPAN), 1)
    delta = (col + (ws - l0)) - (row & (DIL_TQ - 1))
    band = (delta >= -DIL_RADIUS) & (delta <= DIL_RADIUS)
    lane = lax.broadcasted_iota(jnp.int32, (DIL_TQ, LANES), 1)
    lo = lane < HEAD_DIM
    zero = jnp.zeros((DIL_TQ, LANES), BF16)
    for p in range(C_HEADS // 2):
        sl = slice(p * LANES, (p + 1) * LANES)
        q2 = q_ref[:, sl]
        qs = jnp.concatenate([jnp.where(lo, q2, zero), jnp.where(lo, zero, q2)], axis=0)
        kw = k_ref[pl.ds(ws, DIL_SPAN), sl]
        vw = v_ref[pl.ds(ws, DIL_SPAN), sl]
        s = lax.dot_general(qs, kw, (((1,), (1,)), ((), ())), preferred_element_type=F32)
        s = jnp.where(band, s, NEG_BIG)
        m = jnp.max(s, axis=-1, keepdims=True)
        pr = jnp.exp2(s - m)
        l = jnp.sum(pr, axis=-1, keepdims=True)
        o = jnp.dot(pr.astype(BF16), vw, preferred_element_type=F32) / l
        lse2 = jnp.broadcast_to(m + jnp.log2(l), (2 * DIL_TQ, LANES))
        o_ref[:, sl] = jnp.where(lo, o[:DIL_TQ], o[DIL_TQ:])
        lse_ref[:, sl] = jnp.where(lo, lse2[:DIL_TQ], lse2[DIL_TQ:])


def _dilated_branch(qc, kc, vc, dilation):
    b, s, c = qc.shape
    seq = s // dilation
    view = lambda a: a.reshape(b, seq, dilation * c)
    o, lse = pl.pallas_call(
        _dilated_kernel,
        grid=(b, dilation, seq // DIL_TQ),
        in_specs=[pl.BlockSpec((None, DIL_TQ, c), lambda bb, r, i: (bb, i, r)),
                  pl.BlockSpec((None, seq, c), lambda bb, r, i: (bb, 0, r)),
                  pl.BlockSpec((None, seq, c), lambda bb, r, i: (bb, 0, r))],
        out_specs=[pl.BlockSpec((None, DIL_TQ, c), lambda bb, r, i: (bb, i, r)),
                   pl.BlockSpec((None, DIL_TQ, c), lambda bb, r, i: (bb, i, r))],
        out_shape=[jax.ShapeDtypeStruct((b, seq, dilation * c), F32),
                   jax.ShapeDtypeStruct((b, seq, dilation * c), F32)],
        compiler_params=_cparams(("arbitrary", "arbitrary", "arbitrary")),
        name=f"dilated_d{dilation}",
    )(view(qc), view(kc), view(vc))
    return o.reshape(b, s, c), lse.reshape(b, s, c)


def _out_proj_kernel(x_ref, oa_ref, ob_ref, o1_ref, o2_ref, o3_ref, l1_ref, l2_ref, l3_ref,
                     g_ref, w_ref, gate_ref, out_ref):
    l1, l2, l3 = l1_ref[...], l2_ref[...], l3_ref[...]
    mx = jnp.maximum(jnp.maximum(l1, l2), l3)
    w1, w2, w3 = jnp.exp2(l1 - mx), jnp.exp2(l2 - mx), jnp.exp2(l3 - mx)
    tot = w1 + w2 + w3
    oc = (w1 / tot) * o1_ref[...] + (w2 / tot) * o2_ref[...] + (w3 / tot) * o3_ref[...]
    g = g_ref[...]
    y = jnp.concatenate([(_rms(oa_ref[...]) * g[:, :A_OUT]).astype(BF16),
                         (_rms(ob_ref[...]) * g[:, A_OUT:A_OUT + B_OUT]).astype(BF16),
                         (_rms(oc) * g[:, A_OUT + B_OUT:]).astype(BF16)], axis=-1)
    out_ref[...] = x_ref[...] + gate_ref[...] * jnp.dot(y, w_ref[...], preferred_element_type=F32)


def _out_proj(x, oa, ob, o_c, lse_c, gain, w, gate, tm=512):
    b, s, d = x.shape
    row = lambda bb, i: (bb, i, 0)
    const2 = lambda bb, i: (0, 0)
    per_b = lambda bb, i: (bb, 0, 0)
    cspec = pl.BlockSpec((None, tm, C_OUT), row)
    return pl.pallas_call(
        _out_proj_kernel,
        grid=(b, s // tm),
        in_specs=[pl.BlockSpec((None, tm, d), row),
                  pl.BlockSpec((None, tm, A_OUT), row), pl.BlockSpec((None, tm, B_OUT), row),
                  cspec, cspec, cspec, cspec, cspec, cspec,
                  pl.BlockSpec((1, d), const2), pl.BlockSpec(w.shape, const2),
                  pl.BlockSpec((None, 1, d), per_b)],
        out_specs=pl.BlockSpec((None, tm, d), row),
        out_shape=jax.ShapeDtypeStruct((b, s, d), F32),
        compiler_params=_cparams(("arbitrary", "arbitrary")),
        name="out_proj",
    )(x, oa, ob, *o_c, *lse_c, gain, w, gate)


def _mlp_kernel(x_ref, sc_ref, sh_ref, nw_ref, w1_ref, w2_ref, gate_ref, fn_ref, out_ref,
                h_sc, acc_sc, *, final_norm):
    k = pl.program_id(2)

    @pl.when(k == 0)
    def _():
        h = _rms(x_ref[...]) * nw_ref[...]
        h_sc[...] = (h * (1.0 + sc_ref[...]) + sh_ref[...]).astype(BF16)
        acc_sc[...] = jnp.zeros(acc_sc.shape, F32)

    a = jnp.dot(h_sc[...], w1_ref[...], preferred_element_type=F32)
    a = jnp.square(jnp.maximum(a, 0.0))
    acc_sc[...] += jnp.dot(a.astype(BF16), w2_ref[...], preferred_element_type=F32)

    @pl.when(k == pl.num_programs(2) - 1)
    def _():
        y = x_ref[...] + gate_ref[...] * acc_sc[...]
        if final_norm:
            y = _rms(y) * fn_ref[...]
        out_ref[...] = y


def _mlp(x, sc, sh, nw, w1, w2, gate, fn, final_norm, tm=1024, fc=1024):
    b, s, d = x.shape
    ff = w1.shape[1]
    row = lambda bb, i, k: (bb, i, 0)
    const2 = lambda bb, i, k: (0, 0)
    per_b = lambda bb, i, k: (bb, 0, 0)
    return pl.pallas_call(
        functools.partial(_mlp_kernel, final_norm=final_norm),
        grid=(b, s // tm, ff // fc),
        in_specs=[pl.BlockSpec((None, tm, d), row),
                  pl.BlockSpec((None, 1, d), per_b), pl.BlockSpec((None, 1, d), per_b),
                  pl.BlockSpec((1, d), const2),
                  pl.BlockSpec((d, fc), lambda bb, i, k: (0, k)),
                  pl.BlockSpec((fc, d), lambda bb, i, k: (k, 0)),
                  pl.BlockSpec((None, 1, d), per_b),
                  pl.BlockSpec((1, d), const2)],
        out_specs=pl.BlockSpec((None, tm, d), row),
        out_shape=jax.ShapeDtypeStruct((b, s, d), F32),
        scratch_shapes=[pltpu.VMEM((tm, d), BF16), pltpu.VMEM((tm, d), F32)],
        compiler_params=_cparams(("arbitrary", "arbitrary", "arbitrary")),
        name="mlp",
    )(x, sc, sh, nw, w1, w2, gate, fn)


def _rope_tables(s):
    def angles(pos, dim):
        inv = ROPE_THETA ** (-jnp.arange(0, dim, 2, dtype=F32) / dim)
        return pos.astype(F32)[:, None] * inv[None, :]

    t = jnp.arange(s, dtype=jnp.int32)
    n_rows = s // GRID_W
    row_idx = jnp.repeat(jnp.arange(n_rows, dtype=jnp.int32), GRID_W)
    col_idx = jnp.tile(jnp.arange(GRID_W, dtype=jnp.int32), n_rows)
    ang_axial = jnp.concatenate([angles(row_idx, HEAD_DIM // 2), angles(col_idx, HEAD_DIM // 2)], axis=-1)
    ang_mla = angles(t, B_ROPE)
    ang_dil = angles(t, HEAD_DIM)

    def head64(ang):
        c, sn = jnp.cos(ang), jnp.sin(ang)
        return jnp.tile(c, (1, 4)), jnp.tile(jnp.concatenate([-sn, sn], axis=-1), (1, 2))

    c, sn = jnp.cos(ang_mla), jnp.sin(ang_mla)
    ones, zeros32 = jnp.ones((s, B_NOPE), F32), jnp.zeros((s, LANES - B_NOPE - B_ROPE), F32)
    mlc = jnp.concatenate([ones, c, c, zeros32], axis=-1)
    mls = jnp.concatenate([jnp.zeros((s, B_NOPE), F32), -sn, sn, zeros32], axis=-1)
    axc, axs = head64(ang_axial)
    dlc, dls = head64(ang_dil)
    return axc, axs, mlc, mls, dlc, dls


def _arrange_w_in(w_in_l):
    d = w_in_l.shape[0]
    sizes = (A_OUT, 128, 128, B_Q_RANK, B_KV_RANK, B_ROPE, C_OUT, C_OUT, C_OUT)
    offs = np.cumsum((0,) + sizes)
    aq, ak, av, bq, bkv, bkr, cq, ck, cv = [w_in_l[:, offs[j]:offs[j + 1]] for j in range(9)]
    bkr_pad = jnp.concatenate([jnp.zeros((d, B_NOPE), F32), bkr,
                               jnp.zeros((d, LANES - B_NOPE - B_ROPE), F32)], axis=-1)
    return jnp.concatenate([aq, ak, av, bq, bkv, bkr_pad, cq, ck, cv], axis=-1).astype(BF16)


def _arrange_w_uq(w):
    r = w.shape[0]
    w = w.reshape(r, B_HEADS, B_NOPE + B_ROPE)
    w = jnp.concatenate([w, jnp.zeros((r, B_HEADS, LANES - B_NOPE - B_ROPE), F32)], axis=-1)
    return w.reshape(r, B_HEADS * LANES).astype(BF16)


def _arrange_w_ukv(w):
    r = w.shape[0]
    w = w.reshape(r, B_HEADS, B_NOPE + B_V)
    k = jnp.concatenate([w[:, :, :B_NOPE], jnp.zeros((r, B_HEADS, LANES - B_NOPE), F32)], axis=-1)
    v = w[:, :, B_NOPE:]
    return jnp.concatenate([k.reshape(r, B_HEADS * LANES), v.reshape(r, B_HEADS * B_V)],
                           axis=-1).astype(BF16)


def kernel(x, c, w_ada, b_ada, norm_mix, norm_mlp, w_in, a_q_norm, a_k_norm, b_q_norm, b_kv_norm,
           b_w_uq, b_w_ukv, out_gain, w_out, w_ff1, w_ff2, final_norm):
    b, s, d = x.shape
    depth = w_ada.shape[0]
    tables = _rope_tables(s)
    c_pad = jnp.concatenate([c, jnp.zeros((8 - b, d), F32)], axis=0)
    mod = _modulation(c_pad, w_ada, b_ada)[:, :b]
    fn = final_norm.reshape(1, d)
    for l in range(depth):
        sh1, sc1, g1, sh2, sc2, g2 = [mod[l, :, j * d:(j + 1) * d].reshape(b, 1, d) for j in range(N_MOD)]
        qa, ka, va, qb, kb, vb, qc, kc, vc = _in_proj(
            x, sc1, sh1, norm_mix[l].reshape(1, d), _arrange_w_in(w_in[l]),
            jnp.tile(a_q_norm[l], 2).reshape(1, LANES), jnp.tile(a_k_norm[l], 2).reshape(1, LANES),
            b_q_norm[l].reshape(1, B_Q_RANK), b_kv_norm[l].reshape(1, B_KV_RANK),
            _arrange_w_uq(b_w_uq[l]), _arrange_w_ukv(b_w_ukv[l]), tables)
        oa = _attn_a(qa, ka, va)
        ob = _attn_b(qb, kb, vb)
        branches = [_dilated_branch(qc, kc, vc, dil) for _, dil in C_BRANCHES]
        x = _out_proj(x, oa, ob, [o for o, _ in branches], [e for _, e in branches],
                      out_gain[l].reshape(1, d), w_out[l].astype(BF16), g1)
        x = _mlp(x, sc2, sh2, norm_mlp[l].reshape(1, d), w_ff1[l].astype(BF16), w_ff2[l].astype(BF16),
                 g2, fn, final_norm=(l == depth - 1))
    return x
```

```python
import functools
import math

import numpy as np
import jax
import jax.numpy as jnp
from jax import lax
from jax.experimental import pallas as pl
from jax.experimental.pallas import tpu as pltpu

F32 = jnp.float32
BF16 = jnp.bfloat16

D_MODEL = 1024
GRID_W = 64
HEAD_DIM = 64
ROPE_THETA = 10000.0
EPS = 1e-6
A_HEADS, A_KV_HEADS = 6, 2
A_GROUP = A_HEADS // A_KV_HEADS
B_HEADS, B_Q_RANK, B_KV_RANK = 4, 256, 128
B_NOPE, B_ROPE, B_V = 64, 32, 64
C_HEADS = 6
C_BRANCHES = ((128, 1), (512, 4), (2048, 16))
D_FF = 4 * D_MODEL
N_MOD = 6
A_OUT, B_OUT, C_OUT = A_HEADS * HEAD_DIM, B_HEADS * B_V, C_HEADS * HEAD_DIM

LANES = 128
LOG2E = 1.4426950408889634
NEG_BIG = -1e30
VMEM_LIMIT = 52 * 1024 * 1024

COL_AQ, COL_AK, COL_AV = 0, 384, 512
COL_BQ, COL_BKV, COL_BKR = 640, 896, 1024
COL_CQ, COL_CK, COL_CV = 1152, 1536, 1920
IN_COLS_PADDED = 2304


def _cparams(sem):
    return pltpu.CompilerParams(dimension_semantics=sem, vmem_limit_bytes=VMEM_LIMIT)


def _rms(x):
    return x * lax.rsqrt(jnp.mean(x * x, axis=-1, keepdims=True) + EPS)


def _mod_kernel(c_ref, w_ref, b_ref, o_ref):
    c = c_ref[...]
    c_act = c * (1.0 / (1.0 + jnp.exp(-c)))
    o_ref[...] = jnp.dot(c_act.astype(BF16), w_ref[...].astype(BF16),
                         preferred_element_type=F32) + b_ref[...]


def _modulation(c_pad, w_ada, b_ada):
    depth, d, n = w_ada.shape
    tn = 1536
    return pl.pallas_call(
        _mod_kernel,
        grid=(depth, n // tn),
        in_specs=[pl.BlockSpec((8, d), lambda l, j: (0, 0)),
                  pl.BlockSpec((None, d, tn), lambda l, j: (l, 0, j)),
                  pl.BlockSpec((None, 1, tn), lambda l, j: (l, 0, j))],
        out_specs=pl.BlockSpec((None, 8, tn), lambda l, j: (l, 0, j)),
        out_shape=jax.ShapeDtypeStruct((depth, 8, n), F32),
        compiler_params=_cparams(("arbitrary", "arbitrary")),
        name="modulation",
    )(c_pad, w_ada, b_ada.reshape(depth, 1, n))


def _rope(xb, c, s, first, shift):
    up = pltpu.roll(xb, LANES - shift, 1)
    dn = pltpu.roll(xb, shift, 1)
    return xb * c + jnp.where(first, up, dn) * s


def _pair_rms(xb, g, lo):
    sq = xb * xb
    s_lo = jnp.sum(jnp.where(lo, sq, 0.0), axis=-1, keepdims=True)
    s_hi = jnp.sum(jnp.where(lo, 0.0, sq), axis=-1, keepdims=True)
    ms = jnp.where(lo, s_lo, s_hi) * (1.0 / HEAD_DIM)
    return xb * lax.rsqrt(ms + EPS) * g


def _in_proj_kernel(x_ref, sc_ref, sh_ref, nw_ref, w_ref, gq_ref, gk_ref, gbq_ref, gbkv_ref,
                    wuq_ref, wukv_ref, axc_ref, axs_ref, mlc_ref, mls_ref, dlc_ref, dls_ref,
                    qa_ref, ka_ref, va_ref, qb_ref, kb_ref, vb_ref, qc_ref, kc_ref, vc_ref):
    tm = x_ref.shape[0]
    h = _rms(x_ref[...]) * nw_ref[...]
    h = h * (1.0 + sc_ref[...]) + sh_ref[...]
    proj = jnp.dot(h.astype(BF16), w_ref[...], preferred_element_type=F32)

    lane = lax.broadcasted_iota(jnp.int32, (tm, LANES), 1)
    lo = lane < HEAD_DIM
    first32 = (lane & 32) == 0
    first16 = (lane & 16) == 0
    axc, axs = axc_ref[...], axs_ref[...]
    mlc, mls = mlc_ref[...], mls_ref[...]
    dlc, dls = dlc_ref[...], dls_ref[...]
    scale_a = HEAD_DIM ** -0.5 * LOG2E
    scale_b = (B_NOPE + B_ROPE) ** -0.5 * LOG2E

    gq = gq_ref[...]
    for p in range(A_HEADS // 2):
        xb = proj[:, COL_AQ + p * LANES: COL_AQ + (p + 1) * LANES]
        xb = _rope(_pair_rms(xb, gq, lo), axc, axs, first32, 32) * scale_a
        swapped = pltpu.roll(xb, HEAD_DIM, 1)
        for half in range(2):
            head = 2 * p + half
            kv = head // A_GROUP
            src = xb if kv == half else swapped
            keep = lo if kv == 0 else jnp.logical_not(lo)
            qa_ref[head] = jnp.where(keep, src, 0.0).astype(BF16)
    ka = proj[:, COL_AK:COL_AK + LANES]
    ka_ref[...] = _rope(_pair_rms(ka, gk_ref[...], lo), axc, axs, first32, 32).T.astype(BF16)
    va_ref[...] = proj[:, COL_AV:COL_AV + LANES].astype(BF16)

    bq = _rms(proj[:, COL_BQ:COL_BQ + B_Q_RANK]) * gbq_ref[...]
    q_up = jnp.dot(bq.astype(BF16), wuq_ref[...], preferred_element_type=F32)
    bkv = _rms(proj[:, COL_BKV:COL_BKV + B_KV_RANK]) * gbkv_ref[...]
    kv_up = jnp.dot(bkv.astype(BF16), wukv_ref[...], preferred_element_type=F32)
    kr = _rope(proj[:, COL_BKR:COL_BKR + LANES], mlc, mls, first16, 16)
    for hd in range(B_HEADS):
        qh = _rope(q_up[:, hd * LANES:(hd + 1) * LANES], mlc, mls, first16, 16) * scale_b
        qb_ref[hd] = qh.astype(BF16)
        kb_ref[hd] = (kv_up[:, hd * LANES:(hd + 1) * LANES] + kr).T.astype(BF16)
    for p in range(B_HEADS // 2):
        vb_ref[p] = kv_up[:, B_HEADS * LANES + p * LANES: B_HEADS * LANES + (p + 1) * LANES].astype(BF16)

    for p in range(C_HEADS // 2):
        sl = slice(p * LANES, (p + 1) * LANES)
        qc = proj[:, COL_CQ + p * LANES: COL_CQ + (p + 1) * LANES]
        kc = proj[:, COL_CK + p * LANES: COL_CK + (p + 1) * LANES]
        qc_ref[:, sl] = (_rope(qc, dlc, dls, first32, 32) * scale_a).astype(BF16)
        kc_ref[:, sl] = _rope(kc, dlc, dls, first32, 32).astype(BF16)
    vc_ref[...] = proj[:, COL_CV:COL_CV + C_OUT].astype(BF16)


def _in_proj(x, sc, sh, nw, w, gq, gk, gbq, gbkv, wuq, wukv, tables, tm=512):
    b, s, d = x.shape
    row = lambda i, bb: (bb, i, 0)
    const2 = lambda i, bb: (0, 0)
    per_b = lambda i, bb: (bb, 0, 0)
    tab = lambda i, bb: (i, 0)
    headed = lambda i, bb: (bb, 0, i, 0)
    in_specs = [pl.BlockSpec((None, tm, d), row),
                pl.BlockSpec((None, 1, d), per_b), pl.BlockSpec((None, 1, d), per_b),
                pl.BlockSpec((1, d), const2),
                pl.BlockSpec(w.shape, const2),
                pl.BlockSpec((1, LANES), const2), pl.BlockSpec((1, LANES), const2),
                pl.BlockSpec((1, B_Q_RANK), const2), pl.BlockSpec((1, B_KV_RANK), const2),
                pl.BlockSpec(wuq.shape, const2), pl.BlockSpec(wukv.shape, const2)]
    in_specs += [pl.BlockSpec((tm, LANES), tab)] * 6
    out_shape = [jax.ShapeDtypeStruct((b, A_HEADS, s, LANES), BF16),
                 jax.ShapeDtypeStruct((b, LANES, s), BF16),
                 jax.ShapeDtypeStruct((b, s, LANES), BF16),
                 jax.ShapeDtypeStruct((b, B_HEADS, s, LANES), BF16),
                 jax.ShapeDtypeStruct((b, B_HEADS, LANES, s), BF16),
                 jax.ShapeDtypeStruct((b, B_HEADS // 2, s, LANES), BF16),
                 jax.ShapeDtypeStruct((b, s, C_OUT), BF16),
                 jax.ShapeDtypeStruct((b, s, C_OUT), BF16),
                 jax.ShapeDtypeStruct((b, s, C_OUT), BF16)]
    out_specs = [pl.BlockSpec((None, A_HEADS, tm, LANES), headed),
                 pl.BlockSpec((None, LANES, tm), lambda i, bb: (bb, 0, i)),
                 pl.BlockSpec((None, tm, LANES), row),
                 pl.BlockSpec((None, B_HEADS, tm, LANES), headed),
                 pl.BlockSpec((None, B_HEADS, LANES, tm), lambda i, bb: (bb, 0, 0, i)),
                 pl.BlockSpec((None, B_HEADS // 2, tm, LANES), headed),
                 pl.BlockSpec((None, tm, C_OUT), row),
                 pl.BlockSpec((None, tm, C_OUT), row),
                 pl.BlockSpec((None, tm, C_OUT), row)]
    return pl.pallas_call(
        _in_proj_kernel,
        grid=(s // tm, b),
        in_specs=in_specs, out_specs=out_specs, out_shape=out_shape,
        compiler_params=_cparams(("arbitrary", "arbitrary")),
        name="in_proj",
    )(x, sc, sh, nw, w, gq, gk, gbq, gbkv, wuq, wukv, *tables)


def _score_pass(q, k_t, s_slot, m_slot):
    s = jnp.dot(q, k_t, preferred_element_type=F32)
    s_slot[...] = s
    mx = s[:, :LANES]
    for j in range(1, s.shape[1] // LANES):
        mx = jnp.maximum(mx, s[:, j * LANES:(j + 1) * LANES])
    m_slot[...] = jnp.broadcast_to(jnp.max(mx, axis=-1, keepdims=True), mx.shape)


def _value_pass(s_slot, m_slot, v):
    m = m_slot[...]
    ps = [jnp.exp2(s_slot[:, j * LANES:(j + 1) * LANES] - m) for j in range(s_slot.shape[1] // LANES)]
    lsum = ps[0]
    for pj in ps[1:]:
        lsum = lsum + pj
    p = jnp.concatenate([pj.astype(BF16) for pj in ps], axis=-1)
    acc = jnp.dot(p, v, preferred_element_type=F32)
    return acc / jnp.sum(lsum, axis=-1, keepdims=True)


def _pipelined_parts(n_parts, score, value):
    score(0, 0)
    for i in range(n_parts):
        if i + 1 < n_parts:
            score(i + 1, (i + 1) % 2)
        value(i, i % 2)


def _attn_scratch(n_parts, m, s_len):
    return [pltpu.VMEM((2, m, s_len), F32), pltpu.VMEM((2, m, LANES), F32),
            pltpu.VMEM((n_parts, m, LANES), F32)]


def _attn_a_kernel(q_ref, kt_ref, v_ref, o_ref, s_sc, m_sc, o_sc):
    n_heads, tq, _ = q_ref.shape

    def score(i, slot):
        _score_pass(q_ref[i], kt_ref[...], s_sc.at[slot], m_sc.at[slot])

    def value(i, slot):
        o_sc[i] = _value_pass(s_sc.at[slot], m_sc.at[slot], v_ref[...])

    _pipelined_parts(n_heads, score, value)
    lane = lax.broadcasted_iota(jnp.int32, (tq, LANES), 1)
    lo = lane < HEAD_DIM
    for p in range(n_heads // 2):
        halves = []
        for half in range(2):
            head = 2 * p + half
            oh = o_sc[head]
            halves.append(oh if head // A_GROUP == half else pltpu.roll(oh, HEAD_DIM, 1))
        o_ref[:, p * LANES:(p + 1) * LANES] = jnp.where(lo, halves[0], halves[1])


def _attn_a(qa, ka_t, va, tq=512):
    b, nh, s, _ = qa.shape
    return pl.pallas_call(
        _attn_a_kernel,
        grid=(b, s // tq),
        in_specs=[pl.BlockSpec((None, nh, tq, LANES), lambda bb, i: (bb, 0, i, 0)),
                  pl.BlockSpec((None, LANES, s), lambda bb, i: (bb, 0, 0)),
                  pl.BlockSpec((None, s, LANES), lambda bb, i: (bb, 0, 0))],
        out_specs=pl.BlockSpec((None, tq, A_OUT), lambda bb, i: (bb, i, 0)),
        out_shape=jax.ShapeDtypeStruct((b, s, A_OUT), F32),
        scratch_shapes=_attn_scratch(nh, tq, s),
        compiler_params=_cparams(("arbitrary", "arbitrary")),
        name="attn_a",
    )(qa, ka_t, va)


def _attn_b_kernel(q_ref, kt_ref, v_ref, o_ref, s_sc, m_sc, o_sc):
    n_heads, tq, _ = q_ref.shape

    def score(i, slot):
        _score_pass(q_ref[i], kt_ref[i], s_sc.at[slot], m_sc.at[slot])

    def value(i, slot):
        o_sc[i] = _value_pass(s_sc.at[slot], m_sc.at[slot], v_ref[i // 2])

    _pipelined_parts(n_heads, score, value)
    lane = lax.broadcasted_iota(jnp.int32, (tq, LANES), 1)
    lo = lane < B_V
    for p in range(n_heads // 2):
        o_ref[:, p * LANES:(p + 1) * LANES] = jnp.where(lo, o_sc[2 * p], o_sc[2 * p + 1])


def _attn_b(qb, kb_t, vb, tq=512):
    b, nh, s, _ = qb.shape
    return pl.pallas_call(
        _attn_b_kernel,
        grid=(b, s // tq),
        in_specs=[pl.BlockSpec((None, nh, tq, LANES), lambda bb, i: (bb, 0, i, 0)),
                  pl.BlockSpec((None, nh, LANES, s), lambda bb, i: (bb, 0, 0, 0)),
                  pl.BlockSpec((None, nh // 2, s, LANES), lambda bb, i: (bb, 0, 0, 0))],
        out_specs=pl.BlockSpec((None, tq, B_OUT), lambda bb, i: (bb, i, 0)),
        out_shape=jax.ShapeDtypeStruct((b, s, B_OUT), F32),
        scratch_shapes=_attn_scratch(nh, tq, s),
        compiler_params=_cparams(("arbitrary", "arbitrary")),
        name="attn_b",
    )(qb, kb_t, vb)


DIL_TQ = 128
DIL_RADIUS = 64
DIL_SPAN = DIL_TQ + 2 * DIL_RADIUS


def _dilated_kernel(q_ref, k_ref, v_ref, o_ref, lse_ref):
    seq = k_ref.shape[0]
    i = pl.program_id(2)
    l0 = i * DIL_TQ
    ws = pl.multiple_of(jnp.clip(l0 - DIL_RADIUS, 0, seq - DIL_SPAN), DIL_RADIUS)
    row = lax.broadcasted_iota(jnp.int32, (2 * DIL_TQ, DIL_SPAN), 0)
    col = lax.broadcasted_iota(jnp.int32, (2 * DIL_TQ, DIL_SPAN), 1)
    delta = (col + (ws - l0)) - (row & (DIL_TQ - 1))
    band = (delta >= -DIL_RADIUS) & (delta <= DIL_RADIUS)
    lane = lax.broadcasted_iota(jnp.int32, (DIL_TQ, LANES), 1)
    lo = lane < HEAD_DIM
    zero = jnp.zeros((DIL_TQ, LANES), BF16)
    for p in range(C_HEADS // 2):
        sl = slice(p * LANES, (p + 1) * LANES)
        q2 = q_ref[:, sl]
        qs = jnp.concatenate([jnp.where(lo, q2, zero), jnp.where(lo, zero, q2)], axis=0)
        kw = k_ref[pl.ds(ws, DIL_SPAN), sl]
        vw = v_ref[pl.ds(ws, DIL_SPAN), sl]
        s = lax.dot_general(qs, kw, (((1,), (1,)), ((), ())), preferred_element_type=F32)
        s = jnp.where(band, s, NEG_BIG)
        m = jnp.max(s, axis=-1, keepdims=True)
        pr = jnp.exp2(s - m)
        l = jnp.sum(pr, axis=-1, keepdims=True)
        o = jnp.dot(pr.astype(BF16), vw, preferred_element_type=F32) / l
        lse2 = jnp.broadcast_to(m + jnp.log2(l), (2 * DIL_TQ, LANES))
        o_ref[:, sl] = jnp.where(lo, o[:DIL_TQ], o[DIL_TQ:])
        lse_ref[:, sl] = jnp.where(lo, lse2[:DIL_TQ], lse2[DIL_TQ:])


def _dilated_branch(qc, kc, vc, dilation):
    b, s, c = qc.shape
    seq = s // dilation
    view = lambda a: a.reshape(b, seq, dilation * c)
    o, lse = pl.pallas_call(
        _dilated_kernel,
        grid=(b, dilation, seq // DIL_TQ),
        in_specs=[pl.BlockSpec((None, DIL_TQ, c), lambda bb, r, i: (bb, i, r)),
                  pl.BlockSpec((None, seq, c), lambda bb, r, i: (bb, 0, r)),
                  pl.BlockSpec((None, seq, c), lambda bb, r, i: (bb, 0, r))],
        out_specs=[pl.BlockSpec((None, DIL_TQ, c), lambda bb, r, i: (bb, i, r)),
                   pl.BlockSpec((None, DIL_TQ, c), lambda bb, r, i: (bb, i, r))],
        out_shape=[jax.ShapeDtypeStruct((b, seq, dilation * c), F32),
                   jax.ShapeDtypeStruct((b, seq, dilation * c), F32)],
        compiler_params=_cparams(("arbitrary", "arbitrary", "arbitrary")),
        name=f"dilated_d{dilation}",
    )(view(qc), view(kc), view(vc))
    return o.reshape(b, s, c), lse.reshape(b, s, c)


def _out_proj_kernel(x_ref, oa_ref, ob_ref, o1_ref, o2_ref, o3_ref, l1_ref, l2_ref, l3_ref,
                     g_ref, w_ref, gate_ref, out_ref):
    l1, l2, l3 = l1_ref[...], l2_ref[...], l3_ref[...]
    mx = jnp.maximum(jnp.maximum(l1, l2), l3)
    w1, w2, w3 = jnp.exp2(l1 - mx), jnp.exp2(l2 - mx), jnp.exp2(l3 - mx)
    tot = w1 + w2 + w3
    oc = (w1 / tot) * o1_ref[...] + (w2 / tot) * o2_ref[...] + (w3 / tot) * o3_ref[...]
    g = g_ref[...]
    y = jnp.concatenate([(_rms(oa_ref[...]) * g[:, :A_OUT]).astype(BF16),
                         (_rms(ob_ref[...]) * g[:, A_OUT:A_OUT + B_OUT]).astype(BF16),
                         (_rms(oc) * g[:, A_OUT + B_OUT:]).astype(BF16)], axis=-1)
    out_ref[...] = x_ref[...] + gate_ref[...] * jnp.dot(y, w_ref[...], preferred_element_type=F32)


def _out_proj(x, oa, ob, o_c, lse_c, gain, w, gate, tm=512):
    b, s, d = x.shape
    row = lambda bb, i: (bb, i, 0)
    const2 = lambda bb, i: (0, 0)
    per_b = lambda bb, i: (bb, 0, 0)
    cspec = pl.BlockSpec((None, tm, C_OUT), row)
    return pl.pallas_call(
        _out_proj_kernel,
        grid=(b, s // tm),
        in_specs=[pl.BlockSpec((None, tm, d), row),
                  pl.BlockSpec((None, tm, A_OUT), row), pl.BlockSpec((None, tm, B_OUT), row),
                  cspec, cspec, cspec, cspec, cspec, cspec,
                  pl.BlockSpec((1, d), const2), pl.BlockSpec(w.shape, const2),
                  pl.BlockSpec((None, 1, d), per_b)],
        out_specs=pl.BlockSpec((None, tm, d), row),
        out_shape=jax.ShapeDtypeStruct((b, s, d), F32),
        compiler_params=_cparams(("arbitrary", "arbitrary")),
        name="out_proj",
    )(x, oa, ob, *o_c, *lse_c, gain, w, gate)


def _mlp_kernel(x_ref, sc_ref, sh_ref, nw_ref, w1_ref, w2_ref, gate_ref, fn_ref, out_ref,
                h_sc, acc_sc, *, final_norm):
    k = pl.program_id(2)

    @pl.when(k == 0)
    def _():
        h = _rms(x_ref[...]) * nw_ref[...]
        h_sc[...] = (h * (1.0 + sc_ref[...]) + sh_ref[...]).astype(BF16)
        acc_sc[...] = jnp.zeros(acc_sc.shape, F32)

    a = jnp.dot(h_sc[...], w1_ref[...], preferred_element_type=F32)
    a = jnp.square(jnp.maximum(a, 0.0))
    acc_sc[...] += jnp.dot(a.astype(BF16), w2_ref[...], preferred_element_type=F32)

    @pl.when(k == pl.num_programs(2) - 1)
    def _():
        y = x_ref[...] + gate_ref[...] * acc_sc[...]
        if final_norm:
            y = _rms(y) * fn_ref[...]
        out_ref[...] = y


def _mlp(x, sc, sh, nw, w1, w2, gate, fn, final_norm, tm=1024, fc=1024):
    b, s, d = x.shape
    ff = w1.shape[1]
    row = lambda bb, i, k: (bb, i, 0)
    const2 = lambda bb, i, k: (0, 0)
    per_b = lambda bb, i, k: (bb, 0, 0)
    return pl.pallas_call(
        functools.partial(_mlp_kernel, final_norm=final_norm),
        grid=(b, s // tm, ff // fc),
        in_specs=[pl.BlockSpec((None, tm, d), row),
                  pl.BlockSpec((None, 1, d), per_b), pl.BlockSpec((None, 1, d), per_b),
                  pl.BlockSpec((1, d), const2),
                  pl.BlockSpec((d, fc), lambda bb, i, k: (0, k)),
                  pl.BlockSpec((fc, d), lambda bb, i, k: (k, 0)),
                  pl.BlockSpec((None, 1, d), per_b),
                  pl.BlockSpec((1, d), const2)],
        out_specs=pl.BlockSpec((None, tm, d), row),
        out_shape=jax.ShapeDtypeStruct((b, s, d), F32),
        scratch_shapes=[pltpu.VMEM((tm, d), BF16), pltpu.VMEM((tm, d), F32)],
        compiler_params=_cparams(("arbitrary", "arbitrary", "arbitrary")),
        name="mlp",
    )(x, sc, sh, nw, w1, w2, gate, fn)


def _rope_tables(s):
    def angles(pos, dim):
        inv = ROPE_THETA ** (-jnp.arange(0, dim, 2, dtype=F32) / dim)
        return pos.astype(F32)[:, None] * inv[None, :]

    t = jnp.arange(s, dtype=jnp.int32)
    n_rows = s // GRID_W
    row_idx = jnp.repeat(jnp.arange(n_rows, dtype=jnp.int32), GRID_W)
    col_idx = jnp.tile(jnp.arange(GRID_W, dtype=jnp.int32), n_rows)
    ang_axial = jnp.concatenate([angles(row_idx, HEAD_DIM // 2), angles(col_idx, HEAD_DIM // 2)], axis=-1)
    ang_mla = angles(t, B_ROPE)
    ang_dil = angles(t, HEAD_DIM)

    def head64(ang):
        c, sn = jnp.cos(ang), jnp.sin(ang)
        return jnp.tile(c, (1, 4)), jnp.tile(jnp.concatenate([-sn, sn], axis=-1), (1, 2))

    c, sn = jnp.cos(ang_mla), jnp.sin(ang_mla)
    ones, zeros32 = jnp.ones((s, B_NOPE), F32), jnp.zeros((s, LANES - B_NOPE - B_ROPE), F32)
    mlc = jnp.concatenate([ones, c, c, zeros32], axis=-1)
    mls = jnp.concatenate([jnp.zeros((s, B_NOPE), F32), -sn, sn, zeros32], axis=-1)
    axc, axs = head64(ang_axial)
    dlc, dls = head64(ang_dil)
    return axc, axs, mlc, mls, dlc, dls


def _arrange_w_in(w_in_l):
    d = w_in_l.shape[0]
    sizes = (A_OUT, 128, 128, B_Q_RANK, B_KV_RANK, B_ROPE, C_OUT, C_OUT, C_OUT)
    offs = np.cumsum((0,) + sizes)
    aq, ak, av, bq, bkv, bkr, cq, ck, cv = [w_in_l[:, offs[j]:offs[j + 1]] for j in range(9)]
    bkr_pad = jnp.concatenate([jnp.zeros((d, B_NOPE), F32), bkr,
                               jnp.zeros((d, LANES - B_NOPE - B_ROPE), F32)], axis=-1)
    return jnp.concatenate([aq, ak, av, bq, bkv, bkr_pad, cq, ck, cv], axis=-1).astype(BF16)


def _arrange_w_uq(w):
    r = w.shape[0]
    w = w.reshape(r, B_HEADS, B_NOPE + B_ROPE)
    w = jnp.concatenate([w, jnp.zeros((r, B_HEADS, LANES - B_NOPE - B_ROPE), F32)], axis=-1)
    return w.reshape(r, B_HEADS * LANES).astype(BF16)


def _arrange_w_ukv(w):
    r = w.shape[0]
    w = w.reshape(r, B_HEADS, B_NOPE + B_V)
    k = jnp.concatenate([w[:, :, :B_NOPE], jnp.zeros((r, B_HEADS, LANES - B_NOPE), F32)], axis=-1)
    v = w[:, :, B_NOPE:]
    return jnp.concatenate([k.reshape(r, B_HEADS * LANES), v.reshape(r, B_HEADS * B_V)],
                           axis=-1).astype(BF16)


def kernel(x, c, w_ada, b_ada, norm_mix, norm_mlp, w_in, a_q_norm, a_k_norm, b_q_norm, b_kv_norm,
           b_w_uq, b_w_ukv, out_gain, w_out, w_ff1, w_ff2, final_norm):
    b, s, d = x.shape
    depth = w_ada.shape[0]
    tables = _rope_tables(s)
    c_pad = jnp.concatenate([c, jnp.zeros((8 - b, d), F32)], axis=0)
    mod = _modulation(c_pad, w_ada, b_ada)[:, :b]
    fn = final_norm.reshape(1, d)
    for l in range(depth):
        sh1, sc1, g1, sh2, sc2, g2 = [mod[l, :, j * d:(j + 1) * d].reshape(b, 1, d) for j in range(N_MOD)]
        qa, ka, va, qb, kb, vb, qc, kc, vc = _in_proj(
            x, sc1, sh1, norm_mix[l].reshape(1, d), _arrange_w_in(w_in[l]),
            jnp.tile(a_q_norm[l], 2).reshape(1, LANES), jnp.tile(a_k_norm[l], 2).reshape(1, LANES),
            b_q_norm[l].reshape(1, B_Q_RANK), b_kv_norm[l].reshape(1, B_KV_RANK),
            _arrange_w_uq(b_w_uq[l]), _arrange_w_ukv(b_w_ukv[l]), tables)
        oa = _attn_a(qa, ka, va)
        ob = _attn_b(qb, kb, vb)
        branches = [_dilated_branch(qc, kc, vc, dil) for _, dil in C_BRANCHES]
        x = _out_proj(x, oa, ob, [o for o, _ in branches], [e for _, e in branches],
                      out_gain[l].reshape(1, d), w_out[l].astype(BF16), g1)
        x = _mlp(x, sc2, sh2, norm_mlp[l].reshape(1, d), w_ff1[l].astype(BF16), w_ff2[l].astype(BF16),
                 g2, fn, final_norm=(l == depth - 1))
    return x
```

```python
import functools
import math

import numpy as np
import jax
import jax.numpy as jnp
from jax import lax
from jax.experimental import pallas as pl
from jax.experimental.pallas import tpu as pltpu

F32 = jnp.float32
BF16 = jnp.bfloat16

D_MODEL = 1024
GRID_W = 64
HEAD_DIM = 64
ROPE_THETA = 10000.0
EPS = 1e-6
A_HEADS, A_KV_HEADS = 6, 2
A_GROUP = A_HEADS // A_KV_HEADS
B_HEADS, B_Q_RANK, B_KV_RANK = 4, 256, 128
B_NOPE, B_ROPE, B_V = 64, 32, 64
C_HEADS = 6
C_BRANCHES = ((128, 1), (512, 4), (2048, 16))
D_FF = 4 * D_MODEL
N_MOD = 6
A_OUT, B_OUT, C_OUT = A_HEADS * HEAD_DIM, B_HEADS * B_V, C_HEADS * HEAD_DIM

LANES = 128
LOG2E = 1.4426950408889634
NEG_BIG = -1e30
VMEM_LIMIT = 52 * 1024 * 1024

COL_AQ, COL_AK, COL_AV = 0, 384, 512
COL_BQ, COL_BKV, COL_BKR = 640, 896, 1024
COL_CQ, COL_CK, COL_CV = 1152, 1536, 1920
IN_COLS_PADDED = 2304


def _cparams(sem):
    return pltpu.CompilerParams(dimension_semantics=sem, vmem_limit_bytes=VMEM_LIMIT)


def _rms(x):
    return x * lax.rsqrt(jnp.mean(x * x, axis=-1, keepdims=True) + EPS)


def _mod_kernel(c_ref, w_ref, b_ref, o_ref):
    c = c_ref[...]
    c_act = c * (1.0 / (1.0 + jnp.exp(-c)))
    o_ref[...] = jnp.dot(c_act.astype(BF16), w_ref[...].astype(BF16),
                         preferred_element_type=F32) + b_ref[...]


def _modulation(c_pad, w_ada, b_ada):
    depth, d, n = w_ada.shape
    tn = 1536
    return pl.pallas_call(
        _mod_kernel,
        grid=(depth, n // tn),
        in_specs=[pl.BlockSpec((8, d), lambda l, j: (0, 0)),
                  pl.BlockSpec((None, d, tn), lambda l, j: (l, 0, j)),
                  pl.BlockSpec((None, 1, tn), lambda l, j: (l, 0, j))],
        out_specs=pl.BlockSpec((None, 8, tn), lambda l, j: (l, 0, j)),
        out_shape=jax.ShapeDtypeStruct((depth, 8, n), F32),
        compiler_params=_cparams(("arbitrary", "arbitrary")),
        name="modulation",
    )(c_pad, w_ada, b_ada.reshape(depth, 1, n))


def _rope(xb, c, s, first, shift):
    up = pltpu.roll(xb, LANES - shift, 1)
    dn = pltpu.roll(xb, shift, 1)
    return xb * c + jnp.where(first, up, dn) * s


def _pair_rms(xb, g, lo):
    sq = xb * xb
    s_lo = jnp.sum(jnp.where(lo, sq, 0.0), axis=-1, keepdims=True)
    s_hi = jnp.sum(jnp.where(lo, 0.0, sq), axis=-1, keepdims=True)
    ms = jnp.where(lo, s_lo, s_hi) * (1.0 / HEAD_DIM)
    return xb * lax.rsqrt(ms + EPS) * g


def _in_proj_kernel(x_ref, sc_ref, sh_ref, nw_ref, w_ref, gq_ref, gk_ref, gbq_ref, gbkv_ref,
                    wuq_ref, wukv_ref, axc_ref, axs_ref, mlc_ref, mls_ref, dlc_ref, dls_ref,
                    qa_ref, ka_ref, va_ref, qb_ref, kb_ref, vb_ref,
                    qc1_ref, kc1_ref, vc1_ref, qc4_ref, kc4_ref, vc4_ref, qc16_ref, kc16_ref, vc16_ref,
                    perm_sc):
    tm = x_ref.shape[0]
    h = _rms(x_ref[...]) * nw_ref[...]
    h = h * (1.0 + sc_ref[...]) + sh_ref[...]
    proj = jnp.dot(h.astype(BF16), w_ref[...], preferred_element_type=F32)

    lane = lax.broadcasted_iota(jnp.int32, (tm, LANES), 1)
    lo = lane < HEAD_DIM
    first32 = (lane & 32) == 0
    first16 = (lane & 16) == 0
    axc, axs = axc_ref[...], axs_ref[...]
    mlc, mls = mlc_ref[...], mls_ref[...]
    dlc, dls = dlc_ref[...], dls_ref[...]
    scale_a = HEAD_DIM ** -0.5 * LOG2E
    scale_b = (B_NOPE + B_ROPE) ** -0.5 * LOG2E

    gq = gq_ref[...]
    for p in range(A_HEADS // 2):
        xb = proj[:, COL_AQ + p * LANES: COL_AQ + (p + 1) * LANES]
        xb = _rope(_pair_rms(xb, gq, lo), axc, axs, first32, 32) * scale_a
        swapped = pltpu.roll(xb, HEAD_DIM, 1)
        for half in range(2):
            head = 2 * p + half
            kv = head // A_GROUP
            src = xb if kv == half else swapped
            keep = lo if kv == 0 else jnp.logical_not(lo)
            qa_ref[head] = jnp.where(keep, src, 0.0).astype(BF16)
    ka = proj[:, COL_AK:COL_AK + LANES]
    ka_ref[...] = _rope(_pair_rms(ka, gk_ref[...], lo), axc, axs, first32, 32).T.astype(BF16)
    va_ref[...] = proj[:, COL_AV:COL_AV + LANES].astype(BF16)

    bq = _rms(proj[:, COL_BQ:COL_BQ + B_Q_RANK]) * gbq_ref[...]
    q_up = jnp.dot(bq.astype(BF16), wuq_ref[...], preferred_element_type=F32)
    bkv = _rms(proj[:, COL_BKV:COL_BKV + B_KV_RANK]) * gbkv_ref[...]
    kv_up = jnp.dot(bkv.astype(BF16), wukv_ref[...], preferred_element_type=F32)
    kr = _rope(proj[:, COL_BKR:COL_BKR + LANES], mlc, mls, first16, 16)
    for hd in range(B_HEADS):
        qh = _rope(q_up[:, hd * LANES:(hd + 1) * LANES], mlc, mls, first16, 16) * scale_b
        qb_ref[hd] = qh.astype(BF16)
        kb_ref[hd] = (kv_up[:, hd * LANES:(hd + 1) * LANES] + kr).T.astype(BF16)
    for p in range(B_HEADS // 2):
        vb_ref[p] = kv_up[:, B_HEADS * LANES + p * LANES: B_HEADS * LANES + (p + 1) * LANES].astype(BF16)

    for p in range(C_HEADS // 2):
        sl = slice(p * LANES, (p + 1) * LANES)
        qc = proj[:, COL_CQ + p * LANES: COL_CQ + (p + 1) * LANES]
        kc = proj[:, COL_CK + p * LANES: COL_CK + (p + 1) * LANES]
        staged = (_rope(qc, dlc, dls, first32, 32) * scale_a, _rope(kc, dlc, dls, first32, 32),
                  proj[:, COL_CV + p * LANES: COL_CV + (p + 1) * LANES])
        outs = ((qc1_ref, qc4_ref, qc16_ref), (kc1_ref, kc4_ref, kc16_ref), (vc1_ref, vc4_ref, vc16_ref))
        for t, (val, (r1, r4, r16)) in enumerate(zip(staged, outs)):
            perm_sc[t] = val
            r1[:, sl] = val.astype(BF16)
            for dil, ref in ((4, r4), (16, r16)):
                for r in range(dil):
                    ref[r, :, sl] = perm_sc[t, pl.ds(r, tm // dil, stride=dil), :].astype(BF16)


def _in_proj(x, sc, sh, nw, w, gq, gk, gbq, gbkv, wuq, wukv, tables, tm=512):
    b, s, d = x.shape
    row = lambda i, bb: (bb, i, 0)
    const2 = lambda i, bb: (0, 0)
    per_b = lambda i, bb: (bb, 0, 0)
    tab = lambda i, bb: (i, 0)
    headed = lambda i, bb: (bb, 0, i, 0)
    in_specs = [pl.BlockSpec((None, tm, d), row),
                pl.BlockSpec((None, 1, d), per_b), pl.BlockSpec((None, 1, d), per_b),
                pl.BlockSpec((1, d), const2),
                pl.BlockSpec(w.shape, const2),
                pl.BlockSpec((1, LANES), const2), pl.BlockSpec((1, LANES), const2),
                pl.BlockSpec((1, B_Q_RANK), const2), pl.BlockSpec((1, B_KV_RANK), const2),
                pl.BlockSpec(wuq.shape, const2), pl.BlockSpec(wukv.shape, const2)]
    in_specs += [pl.BlockSpec((tm, LANES), tab)] * 6
    out_shape = [jax.ShapeDtypeStruct((b, A_HEADS, s, LANES), BF16),
                 jax.ShapeDtypeStruct((b, LANES, s), BF16),
                 jax.ShapeDtypeStruct((b, s, LANES), BF16),
                 jax.ShapeDtypeStruct((b, B_HEADS, s, LANES), BF16),
                 jax.ShapeDtypeStruct((b, B_HEADS, LANES, s), BF16),
                 jax.ShapeDtypeStruct((b, B_HEADS // 2, s, LANES), BF16)]
    out_shape += [jax.ShapeDtypeStruct((b, s, C_OUT), BF16)] * 3
    out_shape += [jax.ShapeDtypeStruct((b, 4, s // 4, C_OUT), BF16)] * 3
    out_shape += [jax.ShapeDtypeStruct((b, 16, s // 16, C_OUT), BF16)] * 3
    out_specs = [pl.BlockSpec((None, A_HEADS, tm, LANES), headed),
                 pl.BlockSpec((None, LANES, tm), lambda i, bb: (bb, 0, i)),
                 pl.BlockSpec((None, tm, LANES), row),
                 pl.BlockSpec((None, B_HEADS, tm, LANES), headed),
                 pl.BlockSpec((None, B_HEADS, LANES, tm), lambda i, bb: (bb, 0, 0, i)),
                 pl.BlockSpec((None, B_HEADS // 2, tm, LANES), headed)]
    out_specs += [pl.BlockSpec((None, tm, C_OUT), row)] * 3
    out_specs += [pl.BlockSpec((None, 4, tm // 4, C_OUT), headed)] * 3
    out_specs += [pl.BlockSpec((None, 16, tm // 16, C_OUT), headed)] * 3
    return pl.pallas_call(
        _in_proj_kernel,
        grid=(s // tm, b),
        in_specs=in_specs, out_specs=out_specs, out_shape=out_shape,
        scratch_shapes=[pltpu.VMEM((3, tm, LANES), F32)],
        compiler_params=_cparams(("arbitrary", "arbitrary")),
        name="in_proj",
    )(x, sc, sh, nw, w, gq, gk, gbq, gbkv, wuq, wukv, *tables)


def _score_pass(q, k_t, s_slot, m_slot):
    s = jnp.dot(q, k_t, preferred_element_type=F32)
    s_slot[...] = s
    mx = s[:, :LANES]
    for j in range(1, s.shape[1] // LANES):
        mx = jnp.maximum(mx, s[:, j * LANES:(j + 1) * LANES])
    m_slot[...] = jnp.broadcast_to(jnp.max(mx, axis=-1, keepdims=True), mx.shape)


def _value_pass(s_slot, m_slot, v):
    m = m_slot[...]
    ps = [jnp.exp2(s_slot[:, j * LANES:(j + 1) * LANES] - m) for j in range(s_slot.shape[1] // LANES)]
    lsum = ps[0]
    for pj in ps[1:]:
        lsum = lsum + pj
    p = jnp.concatenate([pj.astype(BF16) for pj in ps], axis=-1)
    acc = jnp.dot(p, v, preferred_element_type=F32)
    return acc / jnp.sum(lsum, axis=-1, keepdims=True)


def _pipelined_parts(n_parts, score, value):
    score(0, 0)
    for i in range(n_parts):
        if i + 1 < n_parts:
            score(i + 1, (i + 1) % 2)
        value(i, i % 2)


def _attn_scratch(n_parts, m, s_len):
    return [pltpu.VMEM((2, m, s_len), F32), pltpu.VMEM((2, m, LANES), F32),
            pltpu.VMEM((n_parts, m, LANES), F32)]


def _attn_a_kernel(q_ref, kt_ref, v_ref, o_ref, s_sc, m_sc, o_sc):
    n_heads, tq, _ = q_ref.shape

    def score(i, slot):
        _score_pass(q_ref[i], kt_ref[...], s_sc.at[slot], m_sc.at[slot])

    def value(i, slot):
        o_sc[i] = _value_pass(s_sc.at[slot], m_sc.at[slot], v_ref[...])

    _pipelined_parts(n_heads, score, value)
    lane = lax.broadcasted_iota(jnp.int32, (tq, LANES), 1)
    lo = lane < HEAD_DIM
    for p in range(n_heads // 2):
        halves = []
        for half in range(2):
            head = 2 * p + half
            oh = o_sc[head]
            halves.append(oh if head // A_GROUP == half else pltpu.roll(oh, HEAD_DIM, 1))
        o_ref[:, p * LANES:(p + 1) * LANES] = jnp.where(lo, halves[0], halves[1])


def _attn_a(qa, ka_t, va, tq=512):
    b, nh, s, _ = qa.shape
    return pl.pallas_call(
        _attn_a_kernel,
        grid=(b, s // tq),
        in_specs=[pl.BlockSpec((None, nh, tq, LANES), lambda bb, i: (bb, 0, i, 0)),
                  pl.BlockSpec((None, LANES, s), lambda bb, i: (bb, 0, 0)),
                  pl.BlockSpec((None, s, LANES), lambda bb, i: (bb, 0, 0))],
        out_specs=pl.BlockSpec((None, tq, A_OUT), lambda bb, i: (bb, i, 0)),
        out_shape=jax.ShapeDtypeStruct((b, s, A_OUT), F32),
        scratch_shapes=_attn_scratch(nh, tq, s),
        compiler_params=_cparams(("arbitrary", "arbitrary")),
        name="attn_a",
    )(qa, ka_t, va)


def _attn_b_kernel(q_ref, kt_ref, v_ref, o_ref, s_sc, m_sc, o_sc):
    n_heads, tq, _ = q_ref.shape

    def score(i, slot):
        _score_pass(q_ref[i], kt_ref[i], s_sc.at[slot], m_sc.at[slot])

    def value(i, slot):
        o_sc[i] = _value_pass(s_sc.at[slot], m_sc.at[slot], v_ref[i // 2])

    _pipelined_parts(n_heads, score, value)
    lane = lax.broadcasted_iota(jnp.int32, (tq, LANES), 1)
    lo = lane < B_V
    for p in range(n_heads // 2):
        o_ref[:, p * LANES:(p + 1) * LANES] = jnp.where(lo, o_sc[2 * p], o_sc[2 * p + 1])


def _attn_b(qb, kb_t, vb, tq=512):
    b, nh, s, _ = qb.shape
    return pl.pallas_call(
        _attn_b_kernel,
        grid=(b, s // tq),
        in_specs=[pl.BlockSpec((None, nh, tq, LANES), lambda bb, i: (bb, 0, i, 0)),
                  pl.BlockSpec((None, nh, LANES, s), lambda bb, i: (bb, 0, 0, 0)),
                  pl.BlockSpec((None, nh // 2, s, LANES), lambda bb, i: (bb, 0, 0, 0))],
        out_specs=pl.BlockSpec((None, tq, B_OUT), lambda bb, i: (bb, i, 0)),
        out_shape=jax.ShapeDtypeStruct((b, s, B_OUT), F32),
        scratch_shapes=_attn_scratch(nh, tq, s),
        compiler_params=_cparams(("arbitrary", "arbitrary")),
        name="attn_b",
    )(qb, kb_t, vb)


DIL_TQ = 128
DIL_RADIUS = 64
DIL_SPAN = DIL_TQ + 2 * DIL_RADIUS


def _band_bias(nq, nk, offset):
    row = lax.broadcasted_iota(jnp.int32, (2 * nq, nk), 0)
    col = lax.broadcasted_iota(jnp.int32, (2 * nq, nk), 1)
    delta = (col + offset) - (row & (nq - 1))
    return jnp.where((delta >= -DIL_RADIUS) & (delta <= DIL_RADIUS), 0.0, NEG_BIG).astype(F32)


def _band_block(q2, kw, vw, bias):
    nq = q2.shape[0]
    lo = lax.broadcasted_iota(jnp.int32, (nq, LANES), 1) < HEAD_DIM
    zero = jnp.zeros_like(q2)
    qs = jnp.concatenate([jnp.where(lo, q2, zero), jnp.where(lo, zero, q2)], axis=0)
    s = lax.dot_general(qs, kw, (((1,), (1,)), ((), ())), preferred_element_type=F32)
    s = s + bias
    m = jnp.max(s, axis=-1, keepdims=True)
    pr = jnp.exp2(s - m)
    l = jnp.sum(pr, axis=-1, keepdims=True)
    o = jnp.dot(pr.astype(BF16), vw, preferred_element_type=F32) / l
    lse = jnp.broadcast_to(m + jnp.log2(l), o.shape)
    return jnp.where(lo, o[:nq], o[nq:]), jnp.where(lo, lse[:nq], lse[nq:])


def _merge(o0, l0, o1, l1):
    mx = jnp.maximum(l0, l1)
    w0, w1 = jnp.exp2(l0 - mx), jnp.exp2(l1 - mx)
    tot = w0 + w1
    return (w0 * o0 + w1 * o1) / tot, mx + jnp.log2(tot)


def _dilated_kernel(q1_ref, k1_ref, v1_ref, q4_ref, k4_ref, v4_ref, q16_ref, k16_ref, v16_ref,
                    o_ref, oa_sc, la_sc, ob_sc, lb_sc, bias16_sc, bias_sc):
    s_len = q1_ref.shape[0]
    seq4, seq16 = s_len // 4, s_len // 16

    bias16_sc[...] = _band_bias(seq16, seq16, 0)
    for case in range(3):
        bias_sc[case] = _band_bias(DIL_TQ, DIL_SPAN, -case * DIL_RADIUS)

    def window(l0, seq):
        ws = pl.multiple_of(jnp.clip(l0 - DIL_RADIUS, 0, seq - DIL_SPAN), DIL_RADIUS)
        return ws, bias_sc[(l0 - ws) // DIL_RADIUS]

    def branch16(r, carry):
        o, l = _band_block(q16_ref[r], k16_ref[r], v16_ref[r], bias16_sc[...])
        rows = pl.ds(pl.multiple_of(r * seq16, seq16), seq16)
        oa_sc[rows, :] = o
        la_sc[rows, :] = l
        return carry

    lax.fori_loop(0, 16, branch16, 0, unroll=4)

    for r4 in range(4):
        for a in range(4):
            src = slice((4 * a + r4) * seq16, (4 * a + r4 + 1) * seq16)
            dst = pl.ds(r4 * seq4 + a, seq16, stride=4)
            ob_sc[dst, :] = oa_sc[src, :]
            lb_sc[dst, :] = la_sc[src, :]

    n_blk4 = seq4 // DIL_TQ

    def branch4(j, carry):
        r4, i = j // n_blk4, j % n_blk4
        l0 = pl.multiple_of(i * DIL_TQ, DIL_TQ)
        ws, bias = window(l0, seq4)
        o, l = _band_block(q4_ref[r4, pl.ds(l0, DIL_TQ), :], k4_ref[r4, pl.ds(ws, DIL_SPAN), :],
                           v4_ref[r4, pl.ds(ws, DIL_SPAN), :], bias)
        rows = pl.ds(pl.multiple_of(r4 * seq4 + l0, DIL_TQ), DIL_TQ)
        o, l = _merge(ob_sc[rows, :], lb_sc[rows, :], o, l)
        ob_sc[rows, :] = o
        lb_sc[rows, :] = l
        return carry

    lax.fori_loop(0, 4 * n_blk4, branch4, 0, unroll=8)

    for r4 in range(4):
        src = slice(r4 * seq4, (r4 + 1) * seq4)
        dst = pl.ds(r4, seq4, stride=4)
        oa_sc[dst, :] = ob_sc[src, :]
        la_sc[dst, :] = lb_sc[src, :]

    def branch1(j, carry):
        l0 = pl.multiple_of(j * DIL_TQ, DIL_TQ)
        ws, bias = window(l0, s_len)
        o, l = _band_block(q1_ref[pl.ds(l0, DIL_TQ), :], k1_ref[pl.ds(ws, DIL_SPAN), :],
                           v1_ref[pl.ds(ws, DIL_SPAN), :], bias)
        rows = pl.ds(l0, DIL_TQ)
        o, _ = _merge(oa_sc[rows, :], la_sc[rows, :], o, l)
        o_ref[rows, :] = o
        return carry

    lax.fori_loop(0, s_len // DIL_TQ, branch1, 0, unroll=8)


def _dilated_mixture(qkv1, qkv4, qkv16):
    b, s, c = qkv1[0].shape
    assert tuple(d for _, d in C_BRANCHES) == (1, 4, 16)
    assert all(w // (2 * d) == DIL_RADIUS for w, d in C_BRANCHES)
    spec1 = pl.BlockSpec((None, s, LANES), lambda bb, p: (bb, 0, p))
    spec4 = pl.BlockSpec((None, 4, s // 4, LANES), lambda bb, p: (bb, 0, 0, p))
    spec16 = pl.BlockSpec((None, 16, s // 16, LANES), lambda bb, p: (bb, 0, 0, p))
    return pl.pallas_call(
        _dilated_kernel,
        grid=(b, c // LANES),
        in_specs=[spec1] * 3 + [spec4] * 3 + [spec16] * 3,
        out_specs=pl.BlockSpec((None, s, LANES), lambda bb, p: (bb, 0, p)),
        out_shape=jax.ShapeDtypeStruct((b, s, c), F32),
        scratch_shapes=[pltpu.VMEM((s, LANES), F32)] * 4 + [
            pltpu.VMEM((2 * (s // 16), s // 16), F32), pltpu.VMEM((3, 2 * DIL_TQ, DIL_SPAN), F32)],
        compiler_params=_cparams(("arbitrary", "arbitrary")),
        name="dilated",
    )(*qkv1, *qkv4, *qkv16)


def _out_proj_kernel(x_ref, oa_ref, ob_ref, oc_ref, g_ref, w_ref, gate_ref, out_ref):
    g = g_ref[...]
    y = jnp.concatenate([(_rms(oa_ref[...]) * g[:, :A_OUT]).astype(BF16),
                         (_rms(ob_ref[...]) * g[:, A_OUT:A_OUT + B_OUT]).astype(BF16),
                         (_rms(oc_ref[...]) * g[:, A_OUT + B_OUT:]).astype(BF16)], axis=-1)
    out_ref[...] = x_ref[...] + gate_ref[...] * jnp.dot(y, w_ref[...], preferred_element_type=F32)


def _out_proj(x, oa, ob, oc, gain, w, gate, tm=512):
    b, s, d = x.shape
    row = lambda bb, i: (bb, i, 0)
    const2 = lambda bb, i: (0, 0)
    per_b = lambda bb, i: (bb, 0, 0)
    return pl.pallas_call(
        _out_proj_kernel,
        grid=(b, s // tm),
        in_specs=[pl.BlockSpec((None, tm, d), row),
                  pl.BlockSpec((None, tm, A_OUT), row), pl.BlockSpec((None, tm, B_OUT), row),
                  pl.BlockSpec((None, tm, C_OUT), row),
                  pl.BlockSpec((1, d), const2), pl.BlockSpec(w.shape, const2),
                  pl.BlockSpec((None, 1, d), per_b)],
        out_specs=pl.BlockSpec((None, tm, d), row),
        out_shape=jax.ShapeDtypeStruct((b, s, d), F32),
        compiler_params=_cparams(("arbitrary", "arbitrary")),
        name="out_proj",
    )(x, oa, ob, oc, gain, w, gate)


def _mlp_kernel(x_ref, sc_ref, sh_ref, nw_ref, w1_ref, w2_ref, gate_ref, fn_ref, out_ref,
                h_sc, acc_sc, *, final_norm):
    k = pl.program_id(2)

    @pl.when(k == 0)
    def _():
        h = _rms(x_ref[...]) * nw_ref[...]
        h_sc[...] = (h * (1.0 + sc_ref[...]) + sh_ref[...]).astype(BF16)
        acc_sc[...] = jnp.zeros(acc_sc.shape, F32)

    a = jnp.dot(h_sc[...], w1_ref[...], preferred_element_type=F32)
    a = jnp.square(jnp.maximum(a, 0.0))
    acc_sc[...] += jnp.dot(a.astype(BF16), w2_ref[...], preferred_element_type=F32)

    @pl.when(k == pl.num_programs(2) - 1)
    def _():
        y = x_ref[...] + gate_ref[...] * acc_sc[...]
        if final_norm:
            y = _rms(y) * fn_ref[...]
        out_ref[...] = y


def _mlp(x, sc, sh, nw, w1, w2, gate, fn, final_norm, tm=1024, fc=1024):
    b, s, d = x.shape
    ff = w1.shape[1]
    row = lambda bb, i, k: (bb, i, 0)
    const2 = lambda bb, i, k: (0, 0)
    per_b = lambda bb, i, k: (bb, 0, 0)
    return pl.pallas_call(
        functools.partial(_mlp_kernel, final_norm=final_norm),
        grid=(b, s // tm, ff // fc),
        in_specs=[pl.BlockSpec((None, tm, d), row),
                  pl.BlockSpec((None, 1, d), per_b), pl.BlockSpec((None, 1, d), per_b),
                  pl.BlockSpec((1, d), const2),
                  pl.BlockSpec((d, fc), lambda bb, i, k: (0, k)),
                  pl.BlockSpec((fc, d), lambda bb, i, k: (k, 0)),
                  pl.BlockSpec((None, 1, d), per_b),
                  pl.BlockSpec((1, d), const2)],
        out_specs=pl.BlockSpec((None, tm, d), row),
        out_shape=jax.ShapeDtypeStruct((b, s, d), F32),
        scratch_shapes=[pltpu.VMEM((tm, d), BF16), pltpu.VMEM((tm, d), F32)],
        compiler_params=_cparams(("arbitrary", "arbitrary", "arbitrary")),
        name="mlp",
    )(x, sc, sh, nw, w1, w2, gate, fn)


def _rope_tables(s):
    def angles(pos, dim):
        inv = ROPE_THETA ** (-jnp.arange(0, dim, 2, dtype=F32) / dim)
        return pos.astype(F32)[:, None] * inv[None, :]

    t = jnp.arange(s, dtype=jnp.int32)
    n_rows = s // GRID_W
    row_idx = jnp.repeat(jnp.arange(n_rows, dtype=jnp.int32), GRID_W)
    col_idx = jnp.tile(jnp.arange(GRID_W, dtype=jnp.int32), n_rows)
    ang_axial = jnp.concatenate([angles(row_idx, HEAD_DIM // 2), angles(col_idx, HEAD_DIM // 2)], axis=-1)
    ang_mla = angles(t, B_ROPE)
    ang_dil = angles(t, HEAD_DIM)

    def head64(ang):
        c, sn = jnp.cos(ang), jnp.sin(ang)
        return jnp.tile(c, (1, 4)), jnp.tile(jnp.concatenate([-sn, sn], axis=-1), (1, 2))

    c, sn = jnp.cos(ang_mla), jnp.sin(ang_mla)
    ones, zeros32 = jnp.ones((s, B_NOPE), F32), jnp.zeros((s, LANES - B_NOPE - B_ROPE), F32)
    mlc = jnp.concatenate([ones, c, c, zeros32], axis=-1)
    mls = jnp.concatenate([jnp.zeros((s, B_NOPE), F32), -sn, sn, zeros32], axis=-1)
    axc, axs = head64(ang_axial)
    dlc, dls = head64(ang_dil)
    return axc, axs, mlc, mls, dlc, dls


def _arrange_w_in(w_in_l):
    d = w_in_l.shape[0]
    sizes = (A_OUT, 128, 128, B_Q_RANK, B_KV_RANK, B_ROPE, C_OUT, C_OUT, C_OUT)
    offs = np.cumsum((0,) + sizes)
    aq, ak, av, bq, bkv, bkr, cq, ck, cv = [w_in_l[:, offs[j]:offs[j + 1]] for j in range(9)]
    bkr_pad = jnp.concatenate([jnp.zeros((d, B_NOPE), F32), bkr,
                               jnp.zeros((d, LANES - B_NOPE - B_ROPE), F32)], axis=-1)
    return jnp.concatenate([aq, ak, av, bq, bkv, bkr_pad, cq, ck, cv], axis=-1).astype(BF16)


def _arrange_w_uq(w):
    r = w.shape[0]
    w = w.reshape(r, B_HEADS, B_NOPE + B_ROPE)
    w = jnp.concatenate([w, jnp.zeros((r, B_HEADS, LANES - B_NOPE - B_ROPE), F32)], axis=-1)
    return w.reshape(r, B_HEADS * LANES).astype(BF16)


def _arrange_w_ukv(w):
    r = w.shape[0]
    w = w.reshape(r, B_HEADS, B_NOPE + B_V)
    k = jnp.concatenate([w[:, :, :B_NOPE], jnp.zeros((r, B_HEADS, LANES - B_NOPE), F32)], axis=-1)
    v = w[:, :, B_NOPE:]
    return jnp.concatenate([k.reshape(r, B_HEADS * LANES), v.reshape(r, B_HEADS * B_V)],
                           axis=-1).astype(BF16)


def kernel(x, c, w_ada, b_ada, norm_mix, norm_mlp, w_in, a_q_norm, a_k_norm, b_q_norm, b_kv_norm,
           b_w_uq, b_w_ukv, out_gain, w_out, w_ff1, w_ff2, final_norm):
    b, s, d = x.shape
    depth = w_ada.shape[0]
    tables = _rope_tables(s)
    c_pad = jnp.concatenate([c, jnp.zeros((8 - b, d), F32)], axis=0)
    mod = _modulation(c_pad, w_ada, b_ada)[:, :b]
    fn = final_norm.reshape(1, d)
    for l in range(depth):
        sh1, sc1, g1, sh2, sc2, g2 = [mod[l, :, j * d:(j + 1) * d].reshape(b, 1, d) for j in range(N_MOD)]
        qa, ka, va, qb, kb, vb, *qkv_c = _in_proj(
            x, sc1, sh1, norm_mix[l].reshape(1, d), _arrange_w_in(w_in[l]),
            jnp.tile(a_q_norm[l], 2).reshape(1, LANES), jnp.tile(a_k_norm[l], 2).reshape(1, LANES),
            b_q_norm[l].reshape(1, B_Q_RANK), b_kv_norm[l].reshape(1, B_KV_RANK),
            _arrange_w_uq(b_w_uq[l]), _arrange_w_ukv(b_w_ukv[l]), tables)
        oa = _attn_a(qa, ka, va)
        ob = _attn_b(qb, kb, vb)
        oc = _dilated_mixture(qkv_c[0:3], qkv_c[3:6], qkv_c[6:9])
        x = _out_proj(x, oa, ob, oc, out_gain[l].reshape(1, d), w_out[l].astype(BF16), g1)
        x = _mlp(x, sc2, sh2, norm_mlp[l].reshape(1, d), w_ff1[l].astype(BF16), w_ff2[l].astype(BF16),
                 g2, fn, final_norm=(l == depth - 1))
    return x
```

```python
import functools
import math

import numpy as np
import jax
import jax.numpy as jnp
from jax import lax
from jax.experimental import pallas as pl
from jax.experimental.pallas import tpu as pltpu

F32 = jnp.float32
BF16 = jnp.bfloat16

D_MODEL = 1024
GRID_W = 64
HEAD_DIM = 64
ROPE_THETA = 10000.0
EPS = 1e-6
A_HEADS, A_KV_HEADS = 6, 2
A_GROUP = A_HEADS // A_KV_HEADS
B_HEADS, B_Q_RANK, B_KV_RANK = 4, 256, 128
B_NOPE, B_ROPE, B_V = 64, 32, 64
C_HEADS = 6
C_BRANCHES = ((128, 1), (512, 4), (2048, 16))
D_FF = 4 * D_MODEL
N_MOD = 6
A_OUT, B_OUT, C_OUT = A_HEADS * HEAD_DIM, B_HEADS * B_V, C_HEADS * HEAD_DIM

LANES = 128
LOG2E = 1.4426950408889634
NEG_BIG = -1e30
VMEM_LIMIT = 52 * 1024 * 1024

COL_AQ, COL_AK, COL_AV = 0, 384, 512
COL_BQ, COL_BKV, COL_BKR = 640, 896, 1024
COL_CQ, COL_CK, COL_CV = 1152, 1536, 1920
IN_COLS_PADDED = 2304


def _cparams(sem):
    return pltpu.CompilerParams(dimension_semantics=sem, vmem_limit_bytes=VMEM_LIMIT)


def _rms(x):
    return x * lax.rsqrt(jnp.mean(x * x, axis=-1, keepdims=True) + EPS)


def _mod_kernel(c_ref, w_ref, b_ref, o_ref):
    c = c_ref[...]
    c_act = c * (1.0 / (1.0 + jnp.exp(-c)))
    o_ref[...] = jnp.dot(c_act.astype(BF16), w_ref[...].astype(BF16),
                         preferred_element_type=F32) + b_ref[...]


def _modulation(c_pad, w_ada, b_ada):
    depth, d, n = w_ada.shape
    tn = 1536
    return pl.pallas_call(
        _mod_kernel,
        grid=(depth, n // tn),
        in_specs=[pl.BlockSpec((8, d), lambda l, j: (0, 0)),
                  pl.BlockSpec((None, d, tn), lambda l, j: (l, 0, j)),
                  pl.BlockSpec((None, 1, tn), lambda l, j: (l, 0, j))],
        out_specs=pl.BlockSpec((None, 8, tn), lambda l, j: (l, 0, j)),
        out_shape=jax.ShapeDtypeStruct((depth, 8, n), F32),
        compiler_params=_cparams(("arbitrary", "arbitrary")),
        name="modulation",
    )(c_pad, w_ada, b_ada.reshape(depth, 1, n))


def _rope(xb, c, s, first, shift):
    up = pltpu.roll(xb, LANES - shift, 1)
    dn = pltpu.roll(xb, shift, 1)
    return xb * c + jnp.where(first, up, dn) * s


def _pair_rms(xb, g, lo):
    sq = xb * xb
    s_lo = jnp.sum(jnp.where(lo, sq, 0.0), axis=-1, keepdims=True)
    s_hi = jnp.sum(jnp.where(lo, 0.0, sq), axis=-1, keepdims=True)
    ms = jnp.where(lo, s_lo, s_hi) * (1.0 / HEAD_DIM)
    return xb * lax.rsqrt(ms + EPS) * g


IN_SUB = 256


def _project(x, sc_ref, sh_ref, nw_ref, w_ref, dst):
    h = _rms(x) * nw_ref[...]
    h = h * (1.0 + sc_ref[...]) + sh_ref[...]
    dst[...] = jnp.dot(h.astype(BF16), w_ref[...], preferred_element_type=F32)


def _head_prep(proj, r0, gq_ref, gk_ref, gbq_ref, gbkv_ref, wuq_ref, wukv_ref, tables, outs, perm_sc):
    sub = proj.shape[0]
    rows = slice(r0, r0 + sub)
    axc, axs, mlc, mls, dlc, dls = [t[rows, :] for t in tables]
    (qa_ref, ka_ref, va_ref, qb_ref, kb_ref, vb_ref,
     qc1_ref, kc1_ref, vc1_ref, qc4_ref, kc4_ref, vc4_ref, qc16_ref, kc16_ref, vc16_ref) = outs

    lane = lax.broadcasted_iota(jnp.int32, (sub, LANES), 1)
    lo = lane < HEAD_DIM
    first32 = (lane & 32) == 0
    first16 = (lane & 16) == 0
    scale_a = HEAD_DIM ** -0.5 * LOG2E
    scale_b = (B_NOPE + B_ROPE) ** -0.5 * LOG2E

    gq = gq_ref[...]
    for p in range(A_HEADS // 2):
        xb = proj[:, COL_AQ + p * LANES: COL_AQ + (p + 1) * LANES]
        xb = _rope(_pair_rms(xb, gq, lo), axc, axs, first32, 32) * scale_a
        swapped = pltpu.roll(xb, HEAD_DIM, 1)
        for half in range(2):
            head = 2 * p + half
            kv = head // A_GROUP
            src = xb if kv == half else swapped
            keep = lo if kv == 0 else jnp.logical_not(lo)
            qa_ref[head, rows, :] = jnp.where(keep, src, 0.0).astype(BF16)
    ka = proj[:, COL_AK:COL_AK + LANES]
    ka_ref[:, rows] = _rope(_pair_rms(ka, gk_ref[...], lo), axc, axs, first32, 32).T.astype(BF16)
    va_ref[rows, :] = proj[:, COL_AV:COL_AV + LANES].astype(BF16)

    bq = _rms(proj[:, COL_BQ:COL_BQ + B_Q_RANK]) * gbq_ref[...]
    q_up = jnp.dot(bq.astype(BF16), wuq_ref[...], preferred_element_type=F32)
    bkv = _rms(proj[:, COL_BKV:COL_BKV + B_KV_RANK]) * gbkv_ref[...]
    kv_up = jnp.dot(bkv.astype(BF16), wukv_ref[...], preferred_element_type=F32)
    kr = _rope(proj[:, COL_BKR:COL_BKR + LANES], mlc, mls, first16, 16)
    for hd in range(B_HEADS):
        qh = _rope(q_up[:, hd * LANES:(hd + 1) * LANES], mlc, mls, first16, 16) * scale_b
        qb_ref[hd, rows, :] = qh.astype(BF16)
        kb_ref[hd, :, rows] = (kv_up[:, hd * LANES:(hd + 1) * LANES] + kr).T.astype(BF16)
    for p in range(B_HEADS // 2):
        vb_ref[p, rows, :] = kv_up[:, B_HEADS * LANES + p * LANES:
                                   B_HEADS * LANES + (p + 1) * LANES].astype(BF16)

    for p in range(C_HEADS // 2):
        sl = slice(p * LANES, (p + 1) * LANES)
        qc = proj[:, COL_CQ + p * LANES: COL_CQ + (p + 1) * LANES]
        kc = proj[:, COL_CK + p * LANES: COL_CK + (p + 1) * LANES]
        staged = (_rope(qc, dlc, dls, first32, 32) * scale_a, _rope(kc, dlc, dls, first32, 32),
                  proj[:, COL_CV + p * LANES: COL_CV + (p + 1) * LANES])
        grouped = ((qc1_ref, qc4_ref, qc16_ref), (kc1_ref, kc4_ref, kc16_ref),
                   (vc1_ref, vc4_ref, vc16_ref))
        for t, (val, (r1, r4, r16)) in enumerate(zip(staged, grouped)):
            perm_sc[t] = val
            r1[rows, sl] = val.astype(BF16)
            for dil, ref in ((4, r4), (16, r16)):
                n = sub // dil
                for r in range(dil):
                    ref[r, r0 // dil:r0 // dil + n, sl] = (
                        perm_sc[t, pl.ds(r, n, stride=dil), :].astype(BF16))


def _in_proj_kernel(x_ref, xn_ref, sc_ref, sh_ref, scn_ref, shn_ref, nw_ref, w_ref,
                    gq_ref, gk_ref, gbq_ref, gbkv_ref, wuq_ref, wukv_ref,
                    axc_ref, axs_ref, mlc_ref, mls_ref, dlc_ref, dls_ref, *rest):
    outs, (proj_sc, perm_sc) = rest[:-2], rest[-2:]
    sub = IN_SUB
    tables = (axc_ref, axs_ref, mlc_ref, mls_ref, dlc_ref, dls_ref)
    prep = functools.partial(_head_prep, gq_ref=gq_ref, gk_ref=gk_ref, gbq_ref=gbq_ref,
                             gbkv_ref=gbkv_ref, wuq_ref=wuq_ref, wukv_ref=wukv_ref,
                             tables=tables, outs=outs, perm_sc=perm_sc)
    first_step = jnp.logical_and(pl.program_id(0) == 0, pl.program_id(1) == 0)
    pl.when(first_step)(
        lambda: _project(x_ref[0:sub, :], sc_ref, sh_ref, nw_ref, w_ref, proj_sc.at[0]))
    _project(x_ref[sub:2 * sub, :], sc_ref, sh_ref, nw_ref, w_ref, proj_sc.at[1])
    prep(proj_sc.at[0], 0)
    _project(xn_ref[...], scn_ref, shn_ref, nw_ref, w_ref, proj_sc.at[0])
    prep(proj_sc.at[1], sub)


def _in_proj(x, sc, sh, nw, w, gq, gk, gbq, gbkv, wuq, wukv, tables):
    b, s, d = x.shape
    tm = 2 * IN_SUB
    n_i = s // tm
    row = lambda i, bb: (bb, i, 0)
    const2 = lambda i, bb: (0, 0)
    per_b = lambda i, bb: (bb, 0, 0)
    tab = lambda i, bb: (i, 0)
    headed = lambda i, bb: (bb, 0, i, 0)

    def nxt(i, bb):
        t = jnp.minimum(i * b + bb + 1, n_i * b - 1)
        return t // b, t % b

    def x_next(i, bb):
        ni, nb = nxt(i, bb)
        return nb, 2 * ni, 0

    per_b_next = lambda i, bb: (nxt(i, bb)[1], 0, 0)
    in_specs = [pl.BlockSpec((None, tm, d), row), pl.BlockSpec((None, IN_SUB, d), x_next),
                pl.BlockSpec((None, 1, d), per_b), pl.BlockSpec((None, 1, d), per_b),
                pl.BlockSpec((None, 1, d), per_b_next), pl.BlockSpec((None, 1, d), per_b_next),
                pl.BlockSpec((1, d), const2),
                pl.BlockSpec(w.shape, const2),
                pl.BlockSpec((1, LANES), const2), pl.BlockSpec((1, LANES), const2),
                pl.BlockSpec((1, B_Q_RANK), const2), pl.BlockSpec((1, B_KV_RANK), const2),
                pl.BlockSpec(wuq.shape, const2), pl.BlockSpec(wukv.shape, const2)]
    in_specs += [pl.BlockSpec((tm, LANES), tab)] * 6
    out_shape = [jax.ShapeDtypeStruct((b, A_HEADS, s, LANES), BF16),
                 jax.ShapeDtypeStruct((b, LANES, s), BF16),
                 jax.ShapeDtypeStruct((b, s, LANES), BF16),
                 jax.ShapeDtypeStruct((b, B_HEADS, s, LANES), BF16),
                 jax.ShapeDtypeStruct((b, B_HEADS, LANES, s), BF16),
                 jax.ShapeDtypeStruct((b, B_HEADS // 2, s, LANES), BF16)]
    out_shape += [jax.ShapeDtypeStruct((b, s, C_OUT), BF16)] * 3
    out_shape += [jax.ShapeDtypeStruct((b, 4, s // 4, C_OUT), BF16)] * 3
    out_shape += [jax.ShapeDtypeStruct((b, 16, s // 16, C_OUT), BF16)] * 3
    out_specs = [pl.BlockSpec((None, A_HEADS, tm, LANES), headed),
                 pl.BlockSpec((None, LANES, tm), lambda i, bb: (bb, 0, i)),
                 pl.BlockSpec((None, tm, LANES), row),
                 pl.BlockSpec((None, B_HEADS, tm, LANES), headed),
                 pl.BlockSpec((None, B_HEADS, LANES, tm), lambda i, bb: (bb, 0, 0, i)),
                 pl.BlockSpec((None, B_HEADS // 2, tm, LANES), headed)]
    out_specs += [pl.BlockSpec((None, tm, C_OUT), row)] * 3
    out_specs += [pl.BlockSpec((None, 4, tm // 4, C_OUT), headed)] * 3
    out_specs += [pl.BlockSpec((None, 16, tm // 16, C_OUT), headed)] * 3
    return pl.pallas_call(
        _in_proj_kernel,
        grid=(n_i, b),
        in_specs=in_specs, out_specs=out_specs, out_shape=out_shape,
        scratch_shapes=[pltpu.VMEM((2, IN_SUB, IN_COLS_PADDED), F32), pltpu.VMEM((3, IN_SUB, LANES), F32)],
        compiler_params=_cparams(("arbitrary", "arbitrary")),
        name="in_proj",
    )(x, x, sc, sh, sc, sh, nw, w, gq, gk, gbq, gbkv, wuq, wukv, *tables)


def _score_pass(q, k_t, s_slot, m_slot):
    s = jnp.dot(q, k_t, preferred_element_type=F32)
    s_slot[...] = s
    mx = s[:, :LANES]
    for j in range(1, s.shape[1] // LANES):
        mx = jnp.maximum(mx, s[:, j * LANES:(j + 1) * LANES])
    m_slot[...] = jnp.broadcast_to(jnp.max(mx, axis=-1, keepdims=True), mx.shape)


def _value_pass(s_slot, m_slot, v):
    m = m_slot[...]
    ps = [jnp.exp2(s_slot[:, j * LANES:(j + 1) * LANES] - m) for j in range(s_slot.shape[1] // LANES)]
    lsum = ps[0]
    for pj in ps[1:]:
        lsum = lsum + pj
    p = jnp.concatenate([pj.astype(BF16) for pj in ps], axis=-1)
    acc = jnp.dot(p, v, preferred_element_type=F32)
    return acc / jnp.sum(lsum, axis=-1, keepdims=True)


def _pipelined_parts(n_parts, score, score_next, value):
    assert n_parts % 2 == 0
    first_step = jnp.logical_and(pl.program_id(0) == 0, pl.program_id(1) == 0)
    pl.when(first_step)(lambda: score(0, 0))
    for i in range(n_parts):
        if i + 1 < n_parts:
            score(i + 1, (i + 1) % 2)
        else:
            score_next(0)
        value(i, i % 2)


def _next_step(bb, i, n_i, n_b):
    nxt = jnp.minimum(bb * n_i + i + 1, n_b * n_i - 1)
    return nxt // n_i, nxt % n_i


def _attn_scratch(n_parts, m, s_len):
    return [pltpu.VMEM((2, m, s_len), F32), pltpu.VMEM((2, m, LANES), F32),
            pltpu.VMEM((n_parts, m, LANES), F32)]


def _attn_a_kernel(q_ref, qn_ref, kt_ref, ktn_ref, v_ref, o_ref, s_sc, m_sc, o_sc):
    n_heads, tq, _ = q_ref.shape

    def score(i, slot):
        _score_pass(q_ref[i], kt_ref[...], s_sc.at[slot], m_sc.at[slot])

    def score_next(slot):
        _score_pass(qn_ref[0], ktn_ref[...], s_sc.at[slot], m_sc.at[slot])

    def value(i, slot):
        o_sc[i] = _value_pass(s_sc.at[slot], m_sc.at[slot], v_ref[...])

    _pipelined_parts(n_heads, score, score_next, value)
    lane = lax.broadcasted_iota(jnp.int32, (tq, LANES), 1)
    lo = lane < HEAD_DIM
    for p in range(n_heads // 2):
        halves = []
        for half in range(2):
            head = 2 * p + half
            oh = o_sc[head]
            halves.append(oh if head // A_GROUP == half else pltpu.roll(oh, HEAD_DIM, 1))
        o_ref[:, p * LANES:(p + 1) * LANES] = jnp.where(lo, halves[0], halves[1])


def _attn_a(qa, ka_t, va, tq=512):
    b, nh, s, _ = qa.shape
    n_i = s // tq

    def q_next(bb, i):
        nb, ni = _next_step(bb, i, n_i, b)
        return nb, 0, ni, 0

    return pl.pallas_call(
        _attn_a_kernel,
        grid=(b, n_i),
        in_specs=[pl.BlockSpec((None, nh, tq, LANES), lambda bb, i: (bb, 0, i, 0)),
                  pl.BlockSpec((None, 1, tq, LANES), q_next),
                  pl.BlockSpec((None, LANES, s), lambda bb, i: (bb, 0, 0)),
                  pl.BlockSpec((None, LANES, s), lambda bb, i: (_next_step(bb, i, n_i, b)[0], 0, 0)),
                  pl.BlockSpec((None, s, LANES), lambda bb, i: (bb, 0, 0))],
        out_specs=pl.BlockSpec((None, tq, A_OUT), lambda bb, i: (bb, i, 0)),
        out_shape=jax.ShapeDtypeStruct((b, s, A_OUT), F32),
        scratch_shapes=_attn_scratch(nh, tq, s),
        compiler_params=_cparams(("arbitrary", "arbitrary")),
        name="attn_a",
    )(qa, qa, ka_t, ka_t, va)


def _attn_b_kernel(q_ref, qn_ref, kt_ref, ktn_ref, v_ref, o_ref, s_sc, m_sc, o_sc):
    n_heads, tq, _ = q_ref.shape

    def score(i, slot):
        _score_pass(q_ref[i], kt_ref[i], s_sc.at[slot], m_sc.at[slot])

    def score_next(slot):
        _score_pass(qn_ref[0], ktn_ref[0], s_sc.at[slot], m_sc.at[slot])

    def value(i, slot):
        o_sc[i] = _value_pass(s_sc.at[slot], m_sc.at[slot], v_ref[i // 2])

    _pipelined_parts(n_heads, score, score_next, value)
    lane = lax.broadcasted_iota(jnp.int32, (tq, LANES), 1)
    lo = lane < B_V
    for p in range(n_heads // 2):
        o_ref[:, p * LANES:(p + 1) * LANES] = jnp.where(lo, o_sc[2 * p], o_sc[2 * p + 1])


def _attn_b(qb, kb_t, vb, tq=512):
    b, nh, s, _ = qb.shape
    n_i = s // tq

    def q_next(bb, i):
        nb, ni = _next_step(bb, i, n_i, b)
        return nb, 0, ni, 0

    return pl.pallas_call(
        _attn_b_kernel,
        grid=(b, n_i),
        in_specs=[pl.BlockSpec((None, nh, tq, LANES), lambda bb, i: (bb, 0, i, 0)),
                  pl.BlockSpec((None, 1, tq, LANES), q_next),
                  pl.BlockSpec((None, nh, LANES, s), lambda bb, i: (bb, 0, 0, 0)),
                  pl.BlockSpec((None, 1, LANES, s),
                               lambda bb, i: (_next_step(bb, i, n_i, b)[0], 0, 0, 0)),
                  pl.BlockSpec((None, nh // 2, s, LANES), lambda bb, i: (bb, 0, 0, 0))],
        out_specs=pl.BlockSpec((None, tq, B_OUT), lambda bb, i: (bb, i, 0)),
        out_shape=jax.ShapeDtypeStruct((b, s, B_OUT), F32),
        scratch_shapes=_attn_scratch(nh, tq, s),
        compiler_params=_cparams(("arbitrary", "arbitrary")),
        name="attn_b",
    )(qb, qb, kb_t, kb_t, vb)


DIL_TQ = 128
DIL_RADIUS = 64
DIL_SPAN = DIL_TQ + 2 * DIL_RADIUS


def _band_bias(nq, nk, offset):
    row = lax.broadcasted_iota(jnp.int32, (2 * nq, nk), 0)
    col = lax.broadcasted_iota(jnp.int32, (2 * nq, nk), 1)
    delta = (col + offset) - (row & (nq - 1))
    return jnp.where((delta >= -DIL_RADIUS) & (delta <= DIL_RADIUS), 0.0, NEG_BIG).astype(F32)


def _band_block(q2, kw, vw, bias):
    nq = q2.shape[0]
    lo = lax.broadcasted_iota(jnp.int32, (nq, LANES), 1) < HEAD_DIM
    zero = jnp.zeros_like(q2)
    qs = jnp.concatenate([jnp.where(lo, q2, zero), jnp.where(lo, zero, q2)], axis=0)
    s = lax.dot_general(qs, kw, (((1,), (1,)), ((), ())), preferred_element_type=F32)
    s = s + bias
    m = jnp.max(s, axis=-1, keepdims=True)
    pr = jnp.exp2(s - m)
    l = jnp.sum(pr, axis=-1, keepdims=True)
    o = jnp.dot(pr.astype(BF16), vw, preferred_element_type=F32) / l
    lse = jnp.broadcast_to(m + jnp.log2(l), o.shape)
    return jnp.where(lo, o[:nq], o[nq:]), jnp.where(lo, lse[:nq], lse[nq:])


def _merge(o0, l0, o1, l1):
    mx = jnp.maximum(l0, l1)
    w0, w1 = jnp.exp2(l0 - mx), jnp.exp2(l1 - mx)
    tot = w0 + w1
    return (w0 * o0 + w1 * o1) / tot, mx + jnp.log2(tot)


def _dilated_kernel(q1_ref, k1_ref, v1_ref, q4_ref, k4_ref, v4_ref, q16_ref, k16_ref, v16_ref,
                    o_ref, oa_sc, la_sc, ob_sc, lb_sc, bias16_sc, bias_sc):
    s_len = q1_ref.shape[0]
    seq4, seq16 = s_len // 4, s_len // 16

    bias16_sc[...] = _band_bias(seq16, seq16, 0)
    for case in range(3):
        bias_sc[case] = _band_bias(DIL_TQ, DIL_SPAN, -case * DIL_RADIUS)

    def window(l0, seq):
        ws = pl.multiple_of(jnp.clip(l0 - DIL_RADIUS, 0, seq - DIL_SPAN), DIL_RADIUS)
        return ws, bias_sc[(l0 - ws) // DIL_RADIUS]

    def branch16(r, carry):
        o, l = _band_block(q16_ref[r], k16_ref[r], v16_ref[r], bias16_sc[...])
        rows = pl.ds(pl.multiple_of(r * seq16, seq16), seq16)
        oa_sc[rows, :] = o
        la_sc[rows, :] = l
        return carry

    lax.fori_loop(0, 16, branch16, 0, unroll=4)

    for r4 in range(4):
        for a in range(4):
            src = slice((4 * a + r4) * seq16, (4 * a + r4 + 1) * seq16)
            dst = pl.ds(r4 * seq4 + a, seq16, stride=4)
            ob_sc[dst, :] = oa_sc[src, :]
            lb_sc[dst, :] = la_sc[src, :]

    n_blk4 = seq4 // DIL_TQ

    def branch4(j, carry):
        r4, i = j // n_blk4, j % n_blk4
        l0 = pl.multiple_of(i * DIL_TQ, DIL_TQ)
        ws, bias = window(l0, seq4)
        o, l = _band_block(q4_ref[r4, pl.ds(l0, DIL_TQ), :], k4_ref[r4, pl.ds(ws, DIL_SPAN), :],
                           v4_ref[r4, pl.ds(ws, DIL_SPAN), :], bias)
        rows = pl.ds(pl.multiple_of(r4 * seq4 + l0, DIL_TQ), DIL_TQ)
        o, l = _merge(ob_sc[rows, :], lb_sc[rows, :], o, l)
        ob_sc[rows, :] = o
        lb_sc[rows, :] = l
        return carry

    lax.fori_loop(0, 4 * n_blk4, branch4, 0, unroll=8)

    for r4 in range(4):
        src = slice(r4 * seq4, (r4 + 1) * seq4)
        dst = pl.ds(r4, seq4, stride=4)
        oa_sc[dst, :] = ob_sc[src, :]
        la_sc[dst, :] = lb_sc[src, :]

    def branch1(j, carry):
        l0 = pl.multiple_of(j * DIL_TQ, DIL_TQ)
        ws, bias = window(l0, s_len)
        o, l = _band_block(q1_ref[pl.ds(l0, DIL_TQ), :], k1_ref[pl.ds(ws, DIL_SPAN), :],
                           v1_ref[pl.ds(ws, DIL_SPAN), :], bias)
        rows = pl.ds(l0, DIL_TQ)
        o, _ = _merge(oa_sc[rows, :], la_sc[rows, :], o, l)
        o_ref[rows, :] = o
        return carry

    lax.fori_loop(0, s_len // DIL_TQ, branch1, 0, unroll=8)


def _dilated_mixture(qkv1, qkv4, qkv16):
    b, s, c = qkv1[0].shape
    assert tuple(d for _, d in C_BRANCHES) == (1, 4, 16)
    assert all(w // (2 * d) == DIL_RADIUS for w, d in C_BRANCHES)
    spec1 = pl.BlockSpec((None, s, LANES), lambda bb, p: (bb, 0, p))
    spec4 = pl.BlockSpec((None, 4, s // 4, LANES), lambda bb, p: (bb, 0, 0, p))
    spec16 = pl.BlockSpec((None, 16, s // 16, LANES), lambda bb, p: (bb, 0, 0, p))
    return pl.pallas_call(
        _dilated_kernel,
        grid=(b, c // LANES),
        in_specs=[spec1] * 3 + [spec4] * 3 + [spec16] * 3,
        out_specs=pl.BlockSpec((None, s, LANES), lambda bb, p: (bb, 0, p)),
        out_shape=jax.ShapeDtypeStruct((b, s, c), F32),
        scratch_shapes=[pltpu.VMEM((s, LANES), F32)] * 4 + [
            pltpu.VMEM((2 * (s // 16), s // 16), F32), pltpu.VMEM((3, 2 * DIL_TQ, DIL_SPAN), F32)],
        compiler_params=_cparams(("arbitrary", "arbitrary")),
        name="dilated",
    )(*qkv1, *qkv4, *qkv16)


def _out_proj_kernel(x_ref, oa_ref, ob_ref, oc_ref, g_ref, w_ref, gate_ref, out_ref):
    g = g_ref[...]
    y = jnp.concatenate([(_rms(oa_ref[...]) * g[:, :A_OUT]).astype(BF16),
                         (_rms(ob_ref[...]) * g[:, A_OUT:A_OUT + B_OUT]).astype(BF16),
                         (_rms(oc_ref[...]) * g[:, A_OUT + B_OUT:]).astype(BF16)], axis=-1)
    out_ref[...] = x_ref[...] + gate_ref[...] * jnp.dot(y, w_ref[...], preferred_element_type=F32)


def _out_proj(x, oa, ob, oc, gain, w, gate, tm=512):
    b, s, d = x.shape
    row = lambda bb, i: (bb, i, 0)
    const2 = lambda bb, i: (0, 0)
    per_b = lambda bb, i: (bb, 0, 0)
    return pl.pallas_call(
        _out_proj_kernel,
        grid=(b, s // tm),
        in_specs=[pl.BlockSpec((None, tm, d), row),
                  pl.BlockSpec((None, tm, A_OUT), row), pl.BlockSpec((None, tm, B_OUT), row),
                  pl.BlockSpec((None, tm, C_OUT), row),
                  pl.BlockSpec((1, d), const2), pl.BlockSpec(w.shape, const2),
                  pl.BlockSpec((None, 1, d), per_b)],
        out_specs=pl.BlockSpec((None, tm, d), row),
        out_shape=jax.ShapeDtypeStruct((b, s, d), F32),
        compiler_params=_cparams(("arbitrary", "arbitrary")),
        name="out_proj",
    )(x, oa, ob, oc, gain, w, gate)


def _mlp_kernel(x_ref, sc_ref, sh_ref, nw_ref, w1_ref, w2_ref, gate_ref, fn_ref, out_ref,
                h_sc, acc_sc, *, final_norm):
    k = pl.program_id(2)

    @pl.when(k == 0)
    def _():
        h = _rms(x_ref[...]) * nw_ref[...]
        h_sc[...] = (h * (1.0 + sc_ref[...]) + sh_ref[...]).astype(BF16)
        acc_sc[...] = jnp.zeros(acc_sc.shape, F32)

    a = jnp.dot(h_sc[...], w1_ref[...], preferred_element_type=F32)
    a = jnp.square(jnp.maximum(a, 0.0))
    acc_sc[...] += jnp.dot(a.astype(BF16), w2_ref[...], preferred_element_type=F32)

    @pl.when(k == pl.num_programs(2) - 1)
    def _():
        y = x_ref[...] + gate_ref[...] * acc_sc[...]
        if final_norm:
            y = _rms(y) * fn_ref[...]
        out_ref[...] = y


def _mlp(x, sc, sh, nw, w1, w2, gate, fn, final_norm, tm=1024, fc=1024):
    b, s, d = x.shape
    ff = w1.shape[1]
    row = lambda bb, i, k: (bb, i, 0)
    const2 = lambda bb, i, k: (0, 0)
    per_b = lambda bb, i, k: (bb, 0, 0)
    return pl.pallas_call(
        functools.partial(_mlp_kernel, final_norm=final_norm),
        grid=(b, s // tm, ff // fc),
        in_specs=[pl.BlockSpec((None, tm, d), row),
                  pl.BlockSpec((None, 1, d), per_b), pl.BlockSpec((None, 1, d), per_b),
                  pl.BlockSpec((1, d), const2),
                  pl.BlockSpec((d, fc), lambda bb, i, k: (0, k)),
                  pl.BlockSpec((fc, d), lambda bb, i, k: (k, 0)),
                  pl.BlockSpec((None, 1, d), per_b),
                  pl.BlockSpec((1, d), const2)],
        out_specs=pl.BlockSpec((None, tm, d), row),
        out_shape=jax.ShapeDtypeStruct((b, s, d), F32),
        scratch_shapes=[pltpu.VMEM((tm, d), BF16), pltpu.VMEM((tm, d), F32)],
        compiler_params=_cparams(("arbitrary", "arbitrary", "arbitrary")),
        name="mlp",
    )(x, sc, sh, nw, w1, w2, gate, fn)


def _rope_tables(s):
    def angles(pos, dim):
        inv = ROPE_THETA ** (-jnp.arange(0, dim, 2, dtype=F32) / dim)
        return pos.astype(F32)[:, None] * inv[None, :]

    t = jnp.arange(s, dtype=jnp.int32)
    n_rows = s // GRID_W
    row_idx = jnp.repeat(jnp.arange(n_rows, dtype=jnp.int32), GRID_W)
    col_idx = jnp.tile(jnp.arange(GRID_W, dtype=jnp.int32), n_rows)
    ang_axial = jnp.concatenate([angles(row_idx, HEAD_DIM // 2), angles(col_idx, HEAD_DIM // 2)], axis=-1)
    ang_mla = angles(t, B_ROPE)
    ang_dil = angles(t, HEAD_DIM)

    def head64(ang):
        c, sn = jnp.cos(ang), jnp.sin(ang)
        return jnp.tile(c, (1, 4)), jnp.tile(jnp.concatenate([-sn, sn], axis=-1), (1, 2))

    c, sn = jnp.cos(ang_mla), jnp.sin(ang_mla)
    ones, zeros32 = jnp.ones((s, B_NOPE), F32), jnp.zeros((s, LANES - B_NOPE - B_ROPE), F32)
    mlc = jnp.concatenate([ones, c, c, zeros32], axis=-1)
    mls = jnp.concatenate([jnp.zeros((s, B_NOPE), F32), -sn, sn, zeros32], axis=-1)
    axc, axs = head64(ang_axial)
    dlc, dls = head64(ang_dil)
    return axc, axs, mlc, mls, dlc, dls


def _arrange_w_in(w_in_l):
    d = w_in_l.shape[0]
    sizes = (A_OUT, 128, 128, B_Q_RANK, B_KV_RANK, B_ROPE, C_OUT, C_OUT, C_OUT)
    offs = np.cumsum((0,) + sizes)
    aq, ak, av, bq, bkv, bkr, cq, ck, cv = [w_in_l[:, offs[j]:offs[j + 1]] for j in range(9)]
    bkr_pad = jnp.concatenate([jnp.zeros((d, B_NOPE), F32), bkr,
                               jnp.zeros((d, LANES - B_NOPE - B_ROPE), F32)], axis=-1)
    return jnp.concatenate([aq, ak, av, bq, bkv, bkr_pad, cq, ck, cv], axis=-1).astype(BF16)


def _arrange_w_uq(w):
    r = w.shape[0]
    w = w.reshape(r, B_HEADS, B_NOPE + B_ROPE)
    w = jnp.concatenate([w, jnp.zeros((r, B_HEADS, LANES - B_NOPE - B_ROPE), F32)], axis=-1)
    return w.reshape(r, B_HEADS * LANES).astype(BF16)


def _arrange_w_ukv(w):
    r = w.shape[0]
    w = w.reshape(r, B_HEADS, B_NOPE + B_V)
    k = jnp.concatenate([w[:, :, :B_NOPE], jnp.zeros((r, B_HEADS, LANES - B_NOPE), F32)], axis=-1)
    v = w[:, :, B_NOPE:]
    return jnp.concatenate([k.reshape(r, B_HEADS * LANES), v.reshape(r, B_HEADS * B_V)],
                           axis=-1).astype(BF16)


def kernel(x, c, w_ada, b_ada, norm_mix, norm_mlp, w_in, a_q_norm, a_k_norm, b_q_norm, b_kv_norm,
           b_w_uq, b_w_ukv, out_gain, w_out, w_ff1, w_ff2, final_norm):
    b, s, d = x.shape
    depth = w_ada.shape[0]
    tables = _rope_tables(s)
    c_pad = jnp.concatenate([c, jnp.zeros((8 - b, d), F32)], axis=0)
    mod = _modulation(c_pad, w_ada, b_ada)[:, :b]
    fn = final_norm.reshape(1, d)
    for l in range(depth):
        sh1, sc1, g1, sh2, sc2, g2 = [mod[l, :, j * d:(j + 1) * d].reshape(b, 1, d) for j in range(N_MOD)]
        qa, ka, va, qb, kb, vb, *qkv_c = _in_proj(
            x, sc1, sh1, norm_mix[l].reshape(1, d), _arrange_w_in(w_in[l]),
            jnp.tile(a_q_norm[l], 2).reshape(1, LANES), jnp.tile(a_k_norm[l], 2).reshape(1, LANES),
            b_q_norm[l].reshape(1, B_Q_RANK), b_kv_norm[l].reshape(1, B_KV_RANK),
            _arrange_w_uq(b_w_uq[l]), _arrange_w_ukv(b_w_ukv[l]), tables)
        oa = _attn_a(qa, ka, va)
        ob = _attn_b(qb, kb, vb)
        oc = _dilated_mixture(qkv_c[0:3], qkv_c[3:6], qkv_c[6:9])
        x = _out_proj(x, oa, ob, oc, out_gain[l].reshape(1, d), w_out[l].astype(BF16), g1)
        x = _mlp(x, sc2, sh2, norm_mlp[l].reshape(1, d), w_ff1[l].astype(BF16), w_ff2[l].astype(BF16),
                 g2, fn, final_norm=(l == depth - 1))
    return x
```
